```python
import math
import jax, jax.numpy as jnp
from jax import lax
import numpy as np

D_MODEL = 1024
BATCH = 2
SEQ = 8192
DEPTH = 4
DEC_BATCH = 8
DEC_SEQ = 8192
PAST_LEN = 128

N_HEADS = 8
HEAD_DIM = 128
N_KV_HEADS = 2
GROUP = N_HEADS // N_KV_HEADS
WINDOW = 128
BLK = 128
Q_W = N_HEADS * HEAD_DIM
KV_W = N_KV_HEADS * HEAD_DIM
NUM_BUCKETS = 32
MAX_DISTANCE = 128
D_RNN = 1024
RNN_BLOCKS = 8
RNN_BW = D_RNN // RNN_BLOCKS
CONV_W = 4
CONV_LEFT = 2
LRU_C = 8.0
D_FF = 2816
IN_W = Q_W + 2 * KV_W + 2 * D_RNN + 2 * D_MODEL
SPLIT_POINTS = [Q_W, Q_W + KV_W, Q_W + 2 * KV_W, Q_W + 2 * KV_W + D_RNN,
                Q_W + 2 * KV_W + 2 * D_RNN, Q_W + 2 * KV_W + 2 * D_RNN + D_MODEL]
EPS = 1e-6
NEG_INF = -1e30

kernel_name = "hybrid_bidir_local_gqa_rglru_macaron"


def rms_norm(x, g):
    xf = x.astype(jnp.float32)
    y = xf * lax.rsqrt(jnp.mean(xf * xf, axis=-1, keepdims=True) + EPS)
    return (y * g.astype(jnp.float32)).astype(x.dtype)


def swiglu_ffn(x, w_up, w_down):
    gate, up = jnp.split(x @ w_up, 2, axis=-1)
    return (jax.nn.silu(gate) * up) @ w_down


def t5_buckets(rel):
    n = NUM_BUCKETS // 2
    max_exact = n // 2
    ret = (rel > 0).astype(np.int32) * n
    na = np.abs(rel)
    large = max_exact + (np.log(np.maximum(na, 1) / max_exact)
                         / math.log(MAX_DISTANCE / max_exact) * (n - max_exact)).astype(np.int32)
    large = np.minimum(large, n - 1)
    return ret + np.where(na < max_exact, na, large)


def windowed_attention(q, k, v, sink, rel_table):
    B, S = q.shape[0], q.shape[1]
    nb = S // BLK
    qb = q.reshape(B, nb, BLK, N_KV_HEADS, GROUP, HEAD_DIM)

    def band(t):
        tp = jnp.pad(t, ((0, 0), (BLK, BLK), (0, 0), (0, 0))).reshape(B, nb + 2, BLK, N_KV_HEADS, HEAD_DIM)
        return jnp.concatenate([tp[:, :-2], tp[:, 1:-1], tp[:, 2:]], axis=2)

    kb, vb = band(k), band(v)
    scores = jnp.einsum('bnqkgd,bnskd->bnkgqs', qb, kb).astype(jnp.float32) * (HEAD_DIM ** -0.5)

    rel = np.arange(3 * BLK)[None, :] - BLK - np.arange(BLK)[:, None]
    bias = rel_table.astype(jnp.float32)[t5_buckets(rel)]
    bias = jnp.transpose(bias, (2, 0, 1)).reshape(N_KV_HEADS, GROUP, BLK, 3 * BLK)
    jpos = np.arange(nb)[:, None] * BLK - BLK + np.arange(3 * BLK)[None, :]
    mask = (np.abs(rel) <= WINDOW)[None] & ((jpos >= 0) & (jpos < S))[:, None, :]
    mask = mask[None, :, None, None]

    scores = jnp.where(mask, scores + bias, NEG_INF)
    s_h = sink.astype(jnp.float32).reshape(N_KV_HEADS, GROUP, 1, 1)
    m = jnp.maximum(jnp.max(scores, axis=-1, keepdims=True), s_h)
    p = jnp.exp(scores - m)
    probs = p / (jnp.sum(p, axis=-1, keepdims=True) + jnp.exp(s_h - m))
    out = jnp.einsum('bnkgqs,bnskd->bnqkgd', probs.astype(v.dtype), vb)
    return out.reshape(B, S, Q_W)


def depthwise_conv(x, w, b):
    S = x.shape[1]
    xp = jnp.pad(x, ((0, 0), (CONV_LEFT, CONV_W - 1 - CONV_LEFT), (0, 0)))
    y = b
    for tap in range(CONV_W):
        y = y + xp[:, tap:tap + S] * w[tap]
    return y


def linear_scan(a, b):
    def combine(left, right):
        a1, b1 = left
        a2, b2 = right
        return a1 * a2, a2 * b1 + b2
    _, h = lax.associative_scan(combine, (a, b), axis=1)
    return h


def rg_lru_direction(x, lam, w_a, b_a, w_x, b_x, reverse):
    B, S, _ = x.shape
    xb = x.reshape(B, S, RNN_BLOCKS, RNN_BW)
    r = jax.nn.sigmoid(jnp.einsum('bsnc,ncd->bsnd', xb, w_a.astype(jnp.float32)).reshape(B, S, D_RNN)
                       + b_a.astype(jnp.float32))
    i = jax.nn.sigmoid(jnp.einsum('bsnc,ncd->bsnd', xb, w_x.astype(jnp.float32)).reshape(B, S, D_RNN)
                       + b_x.astype(jnp.float32))
    log_a = -LRU_C * r * jax.nn.softplus(-lam.astype(jnp.float32))
    a = jnp.exp(log_a)
    b = jnp.sqrt(-jnp.expm1(2.0 * log_a)) * (i * x)
    if reverse:
        return linear_scan(a[:, ::-1], b[:, ::-1])[:, ::-1]
    return linear_scan(a, b)


def mixer(h, w_in, conv_w, conv_b, lam, w_a, b_a, w_x, b_x, sink, rel_table, w_br_attn, w_br_rnn, w_out):
    B, S, _ = h.shape
    q, k, v, xr, yr, g_attn, g_rnn = jnp.split(h @ w_in, SPLIT_POINTS, axis=-1)
    attn = windowed_attention(q.reshape(B, S, N_HEADS, HEAD_DIM),
                              k.reshape(B, S, N_KV_HEADS, HEAD_DIM),
                              v.reshape(B, S, N_KV_HEADS, HEAD_DIM), sink, rel_table)
    xc = depthwise_conv(xr, conv_w, conv_b).astype(jnp.float32)
    rec = (rg_lru_direction(xc, lam[0], w_a[0], b_a[0], w_x[0], b_x[0], False)
           + rg_lru_direction(xc, lam[1], w_a[1], b_a[1], w_x[1], b_x[1], True))
    rnn = rec.astype(h.dtype) * jax.nn.gelu(yr)
    merged = jax.nn.sigmoid(g_attn) * (attn @ w_br_attn) + jax.nn.sigmoid(g_rnn) * (rnn @ w_br_rnn)
    return merged @ w_out


def trunk(x, ffn1_norm, ffn1_w_up, ffn1_w_down, mix_norm, w_in, conv_w, conv_b, rg_lambda,
          rg_w_a, rg_b_a, rg_w_x, rg_b_x, attn_sink, rel_bias_table, w_br_attn, w_br_rnn, w_out,
          ffn2_norm, ffn2_w_up, ffn2_w_down, final_norm):
    for l in range(DEPTH):
        x = x + 0.5 * swiglu_ffn(rms_norm(x, ffn1_norm[l]), ffn1_w_up[l], ffn1_w_down[l])
        x = x + mixer(rms_norm(x, mix_norm[l]), w_in[l], conv_w[l], conv_b[l], rg_lambda[l],
                      rg_w_a[l], rg_b_a[l], rg_w_x[l], rg_b_x[l], attn_sink[l], rel_bias_table,
                      w_br_attn[l], w_br_rnn[l], w_out[l])
        x = x + 0.5 * swiglu_ffn(rms_norm(x, ffn2_norm[l]), ffn2_w_up[l], ffn2_w_down[l])
    return rms_norm(x, final_norm)


def setup_inputs(seed: int = 0) -> dict:
    key = jax.random.key(seed)
    ks = jax.random.split(key, 26)
    f32 = jnp.float32

    def nrm(k, shape, scale):
        return jax.random.normal(k, shape, f32) * scale

    def gain(k, shape):
        return 1.0 + 0.05 * jax.random.normal(k, shape, f32)

    a0 = jax.random.uniform(ks[7], (DEPTH, 2, D_RNN), f32, 0.9, 0.999)
    return {
        "x_prompt": nrm(ks[0], (BATCH, SEQ, D_MODEL), 1.0),
        "x_sample": nrm(ks[1], (DEC_BATCH, DEC_SEQ, D_MODEL), 1.0),
        "ffn1_norm": gain(ks[2], (DEPTH, D_MODEL)),
        "ffn1_w_up": nrm(ks[3], (DEPTH, D_MODEL, 2 * D_FF), D_MODEL ** -0.5),
        "ffn1_w_down": nrm(ks[4], (DEPTH, D_FF, D_MODEL), D_FF ** -0.5),
        "mix_norm": gain(ks[5], (DEPTH, D_MODEL)),
        "w_in": nrm(ks[6], (DEPTH, D_MODEL, IN_W), D_MODEL ** -0.5),
        "conv_w": nrm(ks[8], (DEPTH, CONV_W, D_RNN), CONV_W ** -0.5),
        "conv_b": nrm(ks[9], (DEPTH, D_RNN), 0.02),
        "rg_lambda": jnp.log(a0) - jnp.log1p(-a0),
        "rg_w_a": nrm(ks[10], (DEPTH, 2, RNN_BLOCKS, RNN_BW, RNN_BW), RNN_BW ** -0.5),
        "rg_b_a": nrm(ks[11], (DEPTH, 2, D_RNN), 0.02),
        "rg_w_x": nrm(ks[12], (DEPTH, 2, RNN_BLOCKS, RNN_BW, RNN_BW), RNN_BW ** -0.5),
        "rg_b_x": nrm(ks[13], (DEPTH, 2, D_RNN), 0.02),
        "attn_sink": nrm(ks[14], (DEPTH, N_HEADS), 0.5),
        "rel_bias_table": nrm(ks[15], (NUM_BUCKETS, N_HEADS), 0.5),
        "w_br_attn": nrm(ks[16], (DEPTH, Q_W, D_MODEL), Q_W ** -0.5),
        "w_br_rnn": nrm(ks[17], (DEPTH, D_RNN, D_MODEL), D_RNN ** -0.5),
        "w_out": nrm(ks[18], (DEPTH, D_MODEL, D_MODEL), D_MODEL ** -0.5),
        "ffn2_norm": gain(ks[19], (DEPTH, D_MODEL)),
        "ffn2_w_up": nrm(ks[20], (DEPTH, D_MODEL, 2 * D_FF), D_MODEL ** -0.5),
        "ffn2_w_down": nrm(ks[21], (DEPTH, D_FF, D_MODEL), D_FF ** -0.5),
        "final_norm": gain(ks[22], (D_MODEL,)),
    }


def reference(x_prompt, x_sample, ffn1_norm, ffn1_w_up, ffn1_w_down, mix_norm, w_in, conv_w, conv_b,
              rg_lambda, rg_w_a, rg_b_a, rg_w_x, rg_b_x, attn_sink, rel_bias_table, w_br_attn,
              w_br_rnn, w_out, ffn2_norm, ffn2_w_up, ffn2_w_down, final_norm):
    y_prompt = trunk(x_prompt, ffn1_norm, ffn1_w_up, ffn1_w_down, mix_norm, w_in, conv_w, conv_b,
                     rg_lambda, rg_w_a, rg_b_a, rg_w_x, rg_b_x, attn_sink, rel_bias_table, w_br_attn,
                     w_br_rnn, w_out, ffn2_norm, ffn2_w_up, ffn2_w_down, final_norm)
    y_sample = trunk(x_sample, ffn1_norm, ffn1_w_up, ffn1_w_down, mix_norm, w_in, conv_w, conv_b,
                     rg_lambda, rg_w_a, rg_b_a, rg_w_x, rg_b_x, attn_sink, rel_bias_table, w_br_attn,
                     w_br_rnn, w_out, ffn2_norm, ffn2_w_up, ffn2_w_down, final_norm)
    return (y_prompt, y_sample)
```

```python
import functools
import math

import numpy as np
import jax
import jax.numpy as jnp
from jax import lax
from jax.experimental import pallas as pl
from jax.experimental.pallas import tpu as pltpu

F32 = jnp.float32
BF16 = jnp.bfloat16

N_HEADS = 8
HEAD_DIM = 128
N_KV_HEADS = 2
GROUP = N_HEADS // N_KV_HEADS
WINDOW = 128
BLK = 128
NUM_BUCKETS = 32
MAX_DISTANCE = 128
RNN_BLOCKS = 8
RNN_BW = 128
CONV_W = 4
CONV_LEFT = 2
LRU_C = 8.0
EPS = 1e-6
NEG_INF = -1e30

SUBLANES = 8
ROW_TILE = 512
ATTN_TILE = 512
RNN_TILE = 256
FF_CHUNK = 512
VMEM_LIMIT = 56 * 1024 * 1024


def _params(n_axes):
    return pltpu.CompilerParams(
        dimension_semantics=("arbitrary",) * n_axes, vmem_limit_bytes=VMEM_LIMIT)


def _resident(shape):
    zeros = (0,) * len(shape)
    return pl.BlockSpec(shape, lambda *_: zeros, pipeline_mode=pl.Buffered(1))


def _rms(x, g):
    ms = jnp.mean(x * x, axis=-1, keepdims=True)
    return x * lax.rsqrt(ms + EPS) * g


def _sigmoid(x):
    return 0.5 * jnp.tanh(0.5 * x) + 0.5


def _dot(a, b):
    return jnp.dot(a, b, preferred_element_type=F32)


def _ffn_body(x_ref, g_ref, wg_ref, wu_ref, wd_ref, *rest, final):
    if final:
        fg_ref, o_ref, a_scr = rest
    else:
        o_ref, a_scr = rest
    x = x_ref[...]
    h = _rms(x, g_ref[...]).astype(BF16)
    d_ff = wg_ref.shape[1]
    for s in range(0, d_ff, FF_CHUNK):
        e = min(s + FF_CHUNK, d_ff)
        gate = _dot(h, wg_ref[:, s:e])
        up = _dot(h, wu_ref[:, s:e])
        hg = 0.5 * gate
        a_scr[:, s:e] = ((hg + hg * jnp.tanh(hg)) * up).astype(BF16)
    y = x + 0.5 * _dot(a_scr[...], wd_ref[...])
    if final:
        y = _rms(y, fg_ref[...])
    o_ref[...] = y


def _ffn(x, g, wg, wu, wd, final_g=None):
    n, d = x.shape
    d_ff = wg.shape[1]
    final = final_g is not None
    row = pl.BlockSpec((ROW_TILE, d), lambda i: (i, 0))
    in_specs = [row, _resident((1, d)), _resident((d, d_ff)), _resident((d, d_ff)),
                _resident((d_ff, d))]
    args = [x, g, wg, wu, wd]
    if final:
        in_specs.append(_resident((1, d)))
        args.append(final_g)
    return pl.pallas_call(
        functools.partial(_ffn_body, final=final),
        out_shape=jax.ShapeDtypeStruct((n, d), F32),
        grid=(n // ROW_TILE,),
        in_specs=in_specs,
        out_specs=row,
        scratch_shapes=[pltpu.VMEM((ROW_TILE, d_ff), BF16)],
        compiler_params=_params(1),
        name="ffn_final" if final else "ffn",
    )(*args)


def _proj_body(x_ref, g_ref, w_ref, q_ref, k_ref, v_ref, xr_ref, yr_ref, ga_ref, gr_ref):
    h = _rms(x_ref[...], g_ref[...]).astype(BF16)
    s = 0
    for ref, scale in ((q_ref, HEAD_DIM ** -0.5), (k_ref, None), (v_ref, None),
                       (xr_ref, None), (yr_ref, None), (ga_ref, None), (gr_ref, None)):
        e = s + ref.shape[1]
        y = _dot(h, w_ref[:, s:e])
        if scale is not None:
            y = y * scale
        ref[...] = y.astype(ref.dtype)
        s = e


def _proj(x, g, w_in, d_rnn):
    n, d = x.shape
    q_w, kv_w = N_HEADS * HEAD_DIM, N_KV_HEADS * HEAD_DIM
    widths = (q_w, kv_w, kv_w, d_rnn, d_rnn, d, d)
    dtypes = (BF16, BF16, BF16, F32, F32, F32, F32)
    assert sum(widths) == w_in.shape[1]
    return pl.pallas_call(
        _proj_body,
        out_shape=[jax.ShapeDtypeStruct((n, w), t) for w, t in zip(widths, dtypes)],
        grid=(n // ROW_TILE,),
        in_specs=[pl.BlockSpec((ROW_TILE, d), lambda i: (i, 0)), _resident((1, d)),
                  _resident(w_in.shape)],
        out_specs=[pl.BlockSpec((ROW_TILE, w), lambda i: (i, 0)) for w in widths],
        compiler_params=_params(1),
        name="proj",
    )(x, g, w_in)


def _t5_bucket_table():
    rel = np.arange(3 * BLK)[None, :] - BLK - np.arange(BLK)[:, None]
    n = NUM_BUCKETS // 2
    max_exact = n // 2
    ret = (rel > 0).astype(np.int32) * n
    na = np.abs(rel)
    large = max_exact + (np.log(np.maximum(na, 1) / max_exact)
                         / math.log(MAX_DISTANCE / max_exact) * (n - max_exact)).astype(np.int32)
    large = np.minimum(large, n - 1)
    bucket = ret + np.where(na < max_exact, na, large)
    return np.where(na <= WINDOW, bucket, -1).astype(np.int32)


def _attn_body(bucket_ref, table_ref, sink_ref, q_ref, kp_ref, kc_ref, kn_ref,
               vp_ref, vc_ref, vn_ref, o_ref, bias_scr, *, n_tiles):
    b, i = pl.program_id(0), pl.program_id(1)
    sub = ATTN_TILE // BLK
    rows = GROUP * BLK

    @pl.when((b == 0) & (i == 0))
    def _():
        bucket = bucket_ref[...]
        col = lax.broadcasted_iota(jnp.int32, bucket.shape, 1)
        for h in range(N_HEADS):
            acc = jnp.full(bucket.shape, NEG_INF, F32)
            for bk in range(NUM_BUCKETS):
                acc = jnp.where(bucket == bk, table_ref[bk, h], acc)
            g, hh = divmod(h, GROUP)
            r = slice(hh * BLK, (hh + 1) * BLK)
            bias_scr[0, g, r, :] = acc
            bias_scr[1, g, r, :] = jnp.where(col < BLK, NEG_INF, acc)
            bias_scr[2, g, r, :] = jnp.where(col >= 2 * BLK, NEG_INF, acc)

    k_all = jnp.concatenate([kp_ref[0], kc_ref[0], kn_ref[0]], axis=0)
    v_all = jnp.concatenate([vp_ref[0], vc_ref[0], vn_ref[0]], axis=0)
    for j in range(sub):
        if j == 0:
            variant = jnp.where(i == 0, 1, 0)
        elif j == sub - 1:
            variant = jnp.where(i == n_tiles - 1, 2, 0)
        else:
            variant = 0
        for g in range(N_KV_HEADS):
            heads = range(g * GROUP, (g + 1) * GROUP)
            q = jnp.concatenate(
                [q_ref[0, j * BLK:(j + 1) * BLK, h * HEAD_DIM:(h + 1) * HEAD_DIM] for h in heads],
                axis=0)
            k = k_all[j * BLK:(j + 3) * BLK, g * HEAD_DIM:(g + 1) * HEAD_DIM]
            v = v_all[j * BLK:(j + 3) * BLK, g * HEAD_DIM:(g + 1) * HEAD_DIM]
            s = lax.dot_general(q, k, (((1,), (1,)), ((), ())), preferred_element_type=F32)
            s = s + bias_scr[variant, g]
            sink = jnp.concatenate(
                [jnp.full((BLK, 1), sink_ref[h], F32) for h in heads], axis=0)
            m = jnp.maximum(jnp.max(s, axis=-1, keepdims=True), sink)
            p = jnp.exp(s - m)
            denom = jnp.sum(p, axis=-1, keepdims=True) + jnp.exp(sink - m)
            o = _dot(p.astype(BF16), v) * (1.0 / denom)
            for hh, h in enumerate(heads):
                o_ref[0, j * BLK:(j + 1) * BLK, h * HEAD_DIM:(h + 1) * HEAD_DIM] = (
                    o[hh * BLK:(hh + 1) * BLK].astype(o_ref.dtype))


def _attention(q, k, v, sink, rel_table):
    bsz, seq, q_w = q.shape
    kv_w = k.shape[2]
    assert seq % ATTN_TILE == 0 and seq >= 2 * BLK
    n_tiles = seq // ATTN_TILE
    sub = ATTN_TILE // BLK
    n_blk = seq // BLK
    smem = pl.BlockSpec(memory_space=pltpu.SMEM)
    cur = pl.BlockSpec((1, ATTN_TILE, kv_w), lambda b, i: (b, i, 0))
    prev = pl.BlockSpec((1, BLK, kv_w), lambda b, i: (b, jnp.maximum(i * sub - 1, 0), 0))
    nxt = pl.BlockSpec((1, BLK, kv_w), lambda b, i: (b, jnp.minimum((i + 1) * sub, n_blk - 1), 0))
    qo = pl.BlockSpec((1, ATTN_TILE, q_w), lambda b, i: (b, i, 0))
    bucket = jnp.asarray(_t5_bucket_table())
    return pl.pallas_call(
        functools.partial(_attn_body, n_tiles=n_tiles),
        out_shape=jax.ShapeDtypeStruct(q.shape, BF16),
        grid=(bsz, n_tiles),
        in_specs=[_resident(bucket.shape), smem, smem, qo, prev, cur, nxt, prev, cur, nxt],
        out_specs=qo,
        scratch_shapes=[pltpu.VMEM((3, N_KV_HEADS, GROUP * BLK, 3 * BLK), F32)],
        compiler_params=_params(2),
        name="attention",
    )(bucket, rel_table, sink, q, k, k, k, v, v, v)


def _gelu_tanh(x):
    return 0.5 * x * (1.0 + jnp.tanh(math.sqrt(2.0 / math.pi) * (x + 0.044715 * (x * x * x))))


def _rnn_body(xr_ref, xp_ref, xn_ref, cw_ref, cb_ref, lam_ref, w_ref, ba_ref, bx_ref, *rest,
              reverse, n_tiles):
    if reverse:
        fwd_ref, yr_ref, o_ref, ext_scr, h_scr, carry_scr = rest
    else:
        o_ref, ext_scr, h_scr, carry_scr = rest
    t = pl.program_id(1)
    tt = (n_tiles - 1 - t) if reverse else t
    tile = xr_ref.shape[1]
    groups = tile // SUBLANES

    @pl.when(t == 0)
    def _():
        carry_scr[...] = jnp.zeros_like(carry_scr)

    x = xr_ref[0]
    ext_scr[0:SUBLANES, :] = jnp.where(tt == 0, 0.0, xp_ref[0])
    ext_scr[SUBLANES:SUBLANES + tile, :] = x
    ext_scr[SUBLANES + tile:, :] = jnp.where(tt == n_tiles - 1, 0.0, xn_ref[0])
    xc = cb_ref[...]
    for tap in range(CONV_W):
        off = SUBLANES + tap - CONV_LEFT
        xs = x if off == SUBLANES else ext_scr[off:off + tile, :]
        xc = xc + xs * cw_ref[tap:tap + 1, :]

    lam = lam_ref[...]
    neg = -lam
    softplus = jnp.maximum(neg, 0.0) + jnp.log1p(jnp.exp(-jnp.abs(neg)))
    log_a_scale = -LRU_C * softplus

    sub_idx = lax.broadcasted_iota(jnp.int32, (groups, SUBLANES, RNN_BW), 1)
    for n in range(RNN_BLOCKS):
        c = slice(n * RNN_BW, (n + 1) * RNN_BW)
        xcn = xc[:, c]
        pre = _dot(xcn.astype(BF16), w_ref[n])
        r = _sigmoid(pre[:, :RNN_BW] + ba_ref[:, c])
        gate_i = _sigmoid(pre[:, RNN_BW:] + bx_ref[:, c])
        log_a = log_a_scale[:, c] * r
        a = jnp.exp(log_a)
        bb = jnp.sqrt(-jnp.tanh(log_a) * (a * a + 1.0)) * (gate_i * xcn)

        a3 = a.reshape(groups, SUBLANES, RNN_BW)
        b3 = bb.reshape(groups, SUBLANES, RNN_BW)
        for d in (1, 2, 4):
            if reverse:
                keep = sub_idx < SUBLANES - d
                shift = SUBLANES - d
            else:
                keep = sub_idx >= d
                shift = d
            a_sh = jnp.where(keep, pltpu.roll(a3, shift, 1), 1.0)
            b_sh = jnp.where(keep, pltpu.roll(b3, shift, 1), 0.0)
            b3 = b3 + a3 * b_sh
            a3 = a3 * a_sh

        h = carry_scr[:, c]
        order = range(groups - 1, -1, -1) if reverse else range(groups)
        edge = 0 if reverse else SUBLANES - 1
        for gi in order:
            hg = b3[gi] + a3[gi] * h
            h_scr[gi * SUBLANES:(gi + 1) * SUBLANES, c] = hg
            h = jnp.broadcast_to(hg[edge:edge + 1, :], (SUBLANES, RNN_BW))
        carry_scr[:, c] = h

    if reverse:
        rec = fwd_ref[0] + h_scr[...]
        o_ref[0] = (rec * _gelu_tanh(yr_ref[0])).astype(o_ref.dtype)
    else:
        o_ref[0] = h_scr[...]


def _rnn(xr, conv_w, conv_b, lam, w_gate, b_a, b_x, fwd=None, yr=None):
    bsz, seq, d_rnn = xr.shape
    reverse = fwd is not None
    n_tiles = seq // RNN_TILE
    n_grp = seq // SUBLANES
    per = RNN_TILE // SUBLANES

    def tidx(t):
        return (n_tiles - 1 - t) if reverse else t

    main = pl.BlockSpec((1, RNN_TILE, d_rnn), lambda b, t: (b, tidx(t), 0))
    prev = pl.BlockSpec((1, SUBLANES, d_rnn),
                        lambda b, t: (b, jnp.maximum(tidx(t) * per - 1, 0), 0))
    nxt = pl.BlockSpec((1, SUBLANES, d_rnn),
                       lambda b, t: (b, jnp.minimum((tidx(t) + 1) * per, n_grp - 1), 0))
    in_specs = [main, prev, nxt, _resident(conv_w.shape), _resident(conv_b.shape),
                _resident(lam.shape), _resident(w_gate.shape), _resident(b_a.shape),
                _resident(b_x.shape)]
    args = [xr, xr, xr, conv_w, conv_b, lam, w_gate, b_a, b_x]
    if reverse:
        in_specs += [main, main]
        args += [fwd, yr]
    return pl.pallas_call(
        functools.partial(_rnn_body, reverse=reverse, n_tiles=n_tiles),
        out_shape=jax.ShapeDtypeStruct(xr.shape, BF16 if reverse else F32),
        grid=(bsz, n_tiles),
        in_specs=in_specs,
        out_specs=main,
        scratch_shapes=[pltpu.VMEM((RNN_TILE + 2 * SUBLANES, d_rnn), F32),
                        pltpu.VMEM((RNN_TILE, d_rnn), F32),
                        pltpu.VMEM((SUBLANES, d_rnn), F32)],
        compiler_params=_params(2),
        name="rnn_bwd" if reverse else "rnn_fwd",
    )(*args)


def _merge_body(x_ref, attn_ref, rnn_ref, ga_ref, gr_ref, wa_ref, wr_ref, wo_ref, o_ref):
    ya = _dot(attn_ref[...], wa_ref[...])
    yr = _dot(rnn_ref[...], wr_ref[...])
    merged = _sigmoid(ga_ref[...]) * ya + _sigmoid(gr_ref[...]) * yr
    o_ref[...] = x_ref[...] + _dot(merged.astype(BF16), wo_ref[...])


def _merge(x, attn, rnn, ga, gr, wa, wr, wo):
    n, d = x.shape

    def row(w):
        return pl.BlockSpec((ROW_TILE, w), lambda i: (i, 0))

    return pl.pallas_call(
        _merge_body,
        out_shape=jax.ShapeDtypeStruct((n, d), F32),
        grid=(n // ROW_TILE,),
        in_specs=[row(d), row(attn.shape[1]), row(rnn.shape[1]), row(d), row(d),
                  _resident(wa.shape), _resident(wr.shape), _resident(wo.shape)],
        out_specs=row(d),
        compiler_params=_params(1),
        name="merge",
    )(x, attn, rnn, ga, gr, wa, wr, wo)


def _trunk(x, p):
    bsz, seq, d = x.shape
    n = bsz * seq
    depth = p["w_in"].shape[0]
    d_rnn = p["conv_w"].shape[2]
    x = x.reshape(n, d)
    for l in range(depth):
        x = _ffn(x, p["ffn1_norm"][l], p["ffn1_wg"][l], p["ffn1_wu"][l], p["ffn1_wd"][l])
        q, k, v, xr, yr, ga, gr = _proj(x, p["mix_norm"][l], p["w_in"][l], d_rnn)
        attn = _attention(q.reshape(bsz, seq, -1), k.reshape(bsz, seq, -1),
                          v.reshape(bsz, seq, -1), p["attn_sink"][l], p["rel_bias_table"])
        xr3 = xr.reshape(bsz, seq, d_rnn)
        rnn_args = (xr3, p["conv_w"][l], p["conv_b"][l])
        fwd = _rnn(*rnn_args, p["lam"][l, 0], p["w_gate"][l, 0], p["b_a"][l, 0], p["b_x"][l, 0])
        rnn = _rnn(*rnn_args, p["lam"][l, 1], p["w_gate"][l, 1], p["b_a"][l, 1], p["b_x"][l, 1],
                   fwd=fwd, yr=yr.reshape(bsz, seq, d_rnn))
        x = _merge(x, attn.reshape(n, -1), rnn.reshape(n, d_rnn), ga, gr,
                   p["w_br_attn"][l], p["w_br_rnn"][l], p["w_out"][l])
        x = _ffn(x, p["ffn2_norm"][l], p["ffn2_wg"][l], p["ffn2_wu"][l], p["ffn2_wd"][l],
                 final_g=p["final_norm"] if l == depth - 1 else None)
    return x.reshape(bsz, seq, d)


def kernel(x_prompt, x_sample, ffn1_norm, ffn1_w_up, ffn1_w_down, mix_norm, w_in, conv_w, conv_b,
           rg_lambda, rg_w_a, rg_b_a, rg_w_x, rg_b_x, attn_sink, rel_bias_table, w_br_attn,
           w_br_rnn, w_out, ffn2_norm, ffn2_w_up, ffn2_w_down, final_norm):
    depth, d = ffn1_norm.shape
    d_ff = ffn1_w_down.shape[1]
    d_rnn = conv_w.shape[2]
    p = {
        "ffn1_norm": ffn1_norm.reshape(depth, 1, d),
        "ffn1_wg": ffn1_w_up[:, :, :d_ff].astype(BF16),
        "ffn1_wu": ffn1_w_up[:, :, d_ff:].astype(BF16),
        "ffn1_wd": ffn1_w_down.astype(BF16),
        "mix_norm": mix_norm.reshape(depth, 1, d),
        "w_in": w_in.astype(BF16),
        "conv_w": conv_w,
        "conv_b": conv_b.reshape(depth, 1, d_rnn),
        "lam": rg_lambda.reshape(depth, 2, 1, d_rnn),
        "w_gate": jnp.concatenate([rg_w_a, rg_w_x], axis=-1).astype(BF16),
        "b_a": rg_b_a.reshape(depth, 2, 1, d_rnn),
        "b_x": rg_b_x.reshape(depth, 2, 1, d_rnn),
        "attn_sink": attn_sink,
        "rel_bias_table": rel_bias_table,
        "w_br_attn": w_br_attn.astype(BF16),
        "w_br_rnn": w_br_rnn.astype(BF16),
        "w_out": w_out.astype(BF16),
        "ffn2_norm": ffn2_norm.reshape(depth, 1, d),
        "ffn2_wg": ffn2_w_up[:, :, :d_ff].astype(BF16),
        "ffn2_wu": ffn2_w_up[:, :, d_ff:].astype(BF16),
        "ffn2_wd": ffn2_w_down.astype(BF16),
        "final_norm": final_norm.reshape(1, d),
    }
    return (_trunk(x_prompt, p), _trunk(x_sample, p))
```

```python
import functools
import math

import numpy as np
import jax
import jax.numpy as jnp
from jax import lax
from jax.experimental import pallas as pl
from jax.experimental.pallas import tpu as pltpu

F32 = jnp.float32
BF16 = jnp.bfloat16

N_HEADS = 8
HEAD_DIM = 128
N_KV_HEADS = 2
GROUP = N_HEADS // N_KV_HEADS
WINDOW = 128
BLK = 128
NUM_BUCKETS = 32
MAX_DISTANCE = 128
RNN_BLOCKS = 8
RNN_BW = 128
CONV_W = 4
CONV_LEFT = 2
LRU_C = 8.0
EPS = 1e-6
NEG_INF = -1e30
LOG2E = math.log2(math.e)

SUBLANES = 8
LANES = 128
ROW_TILE = 512
ATTN_TILE = 512
RNN_TILE = 256
FF_CHUNK = 512
VMEM_LIMIT = 56 * 1024 * 1024

assert RNN_BLOCKS == SUBLANES and RNN_BW == LANES


def _params(n_axes):
    return pltpu.CompilerParams(
        dimension_semantics=("arbitrary",) * n_axes, vmem_limit_bytes=VMEM_LIMIT)


def _resident(shape):
    zeros = (0,) * len(shape)
    return pl.BlockSpec(shape, lambda *_: zeros, pipeline_mode=pl.Buffered(1))


def _rms(x, g):
    ms = jnp.mean(x * x, axis=-1, keepdims=True)
    return x * lax.rsqrt(ms + EPS) * g


def _sigmoid(x):
    return 0.5 * jnp.tanh(0.5 * x) + 0.5


def _gelu_tanh(x):
    return 0.5 * x * (1.0 + jnp.tanh(math.sqrt(2.0 / math.pi) * (x + 0.044715 * (x * x * x))))


def _dot(a, b):
    return jnp.dot(a, b, preferred_element_type=F32)


def _block_rows(ref, blk, n):
    return ref.at[pl.ds(blk, n, stride=SUBLANES), :]


def _ffn_body(x_ref, g_ref, wg_ref, wu_ref, wd_ref, *rest, final):
    if final:
        fg_ref, o_ref, a_scr = rest
    else:
        o_ref, a_scr = rest
    x = x_ref[...]
    h = _rms(x, g_ref[...]).astype(BF16)
    d_ff = wg_ref.shape[1]
    for s in range(0, d_ff, FF_CHUNK):
        e = min(s + FF_CHUNK, d_ff)
        gate = _dot(h, wg_ref[:, s:e])
        up = _dot(h, wu_ref[:, s:e])
        hg = 0.5 * gate
        a_scr[:, s:e] = ((hg + hg * jnp.tanh(hg)) * up).astype(BF16)
    y = x + 0.5 * _dot(a_scr[...], wd_ref[...])
    if final:
        y = _rms(y, fg_ref[...])
    o_ref[...] = y


def _ffn(x, g, wg, wu, wd, final_g=None):
    n, d = x.shape
    d_ff = wg.shape[1]
    final = final_g is not None
    row = pl.BlockSpec((ROW_TILE, d), lambda i: (i, 0))
    in_specs = [row, _resident((1, d)), _resident((d, d_ff)), _resident((d, d_ff)),
                _resident((d_ff, d))]
    args = [x, g, wg, wu, wd]
    if final:
        in_specs.append(_resident((1, d)))
        args.append(final_g)
    return pl.pallas_call(
        functools.partial(_ffn_body, final=final),
        out_shape=jax.ShapeDtypeStruct((n, d), F32),
        grid=(n // ROW_TILE,),
        in_specs=in_specs,
        out_specs=row,
        scratch_shapes=[pltpu.VMEM((ROW_TILE, d_ff), BF16)],
        compiler_params=_params(1),
        name="ffn_final" if final else "ffn",
    )(*args)


def _proj_body(x_ref, g_ref, w_ref, q_ref, k_ref, v_ref, xr_ref, gy_ref, ga_ref, gr_ref):
    h = _rms(x_ref[...], g_ref[...]).astype(BF16)
    rows = x_ref.shape[0]
    s = 0
    for ref in (q_ref, k_ref, v_ref, xr_ref, gy_ref, ga_ref, gr_ref):
        width = RNN_BLOCKS * RNN_BW if ref is xr_ref else ref.shape[1]
        e = s + width
        y = _dot(h, w_ref[:, s:e])
        if ref is q_ref:
            y = y * (HEAD_DIM ** -0.5 * LOG2E)
        if ref is gy_ref:
            y = _gelu_tanh(y)
        if ref is xr_ref:
            for blk in range(RNN_BLOCKS):
                _block_rows(xr_ref, blk, rows)[...] = y[:, blk * LANES:(blk + 1) * LANES]
        else:
            ref[...] = y.astype(ref.dtype)
        s = e


def _proj(x, g, w_in, d_rnn):
    n, d = x.shape
    q_w, kv_w = N_HEADS * HEAD_DIM, N_KV_HEADS * HEAD_DIM
    assert d_rnn == RNN_BLOCKS * RNN_BW == q_w
    assert q_w + 2 * kv_w + 2 * d_rnn + 2 * d == w_in.shape[1]

    def out(width, dtype):
        return (jax.ShapeDtypeStruct((n, width), dtype),
                pl.BlockSpec((ROW_TILE, width), lambda i: (i, 0)))

    outs = [out(q_w, BF16), out(kv_w, BF16), out(kv_w, BF16),
            (jax.ShapeDtypeStruct((n * SUBLANES, LANES), F32),
             pl.BlockSpec((ROW_TILE * SUBLANES, LANES), lambda i: (i, 0))),
            out(d_rnn, F32), out(d, F32), out(d, F32)]
    return pl.pallas_call(
        _proj_body,
        out_shape=[o[0] for o in outs],
        grid=(n // ROW_TILE,),
        in_specs=[pl.BlockSpec((ROW_TILE, d), lambda i: (i, 0)), _resident((1, d)),
                  _resident(w_in.shape)],
        out_specs=[o[1] for o in outs],
        compiler_params=_params(1),
        name="proj",
    )(x, g, w_in)


def _t5_bucket_table():
    rel = np.arange(3 * BLK)[:, None] - BLK - np.arange(BLK)[None, :]
    n = NUM_BUCKETS // 2
    max_exact = n // 2
    ret = (rel > 0).astype(np.int32) * n
    na = np.abs(rel)
    large = max_exact + (np.log(np.maximum(na, 1) / max_exact)
                         / math.log(MAX_DISTANCE / max_exact) * (n - max_exact)).astype(np.int32)
    large = np.minimum(large, n - 1)
    bucket = ret + np.where(na < max_exact, na, large)
    return np.where(na <= WINDOW, bucket, -1).astype(np.int32)


def _attn_body(bucket_ref, table_ref, sink_ref, q_ref, kp_ref, kc_ref, kn_ref,
               vp_ref, vc_ref, vn_ref, o_ref, bias_scr, *, n_tiles):
    b, i = pl.program_id(0), pl.program_id(1)
    sub = ATTN_TILE // BLK

    @pl.when((b == 0) & (i == 0))
    def _():
        bucket = bucket_ref[...]
        row = lax.broadcasted_iota(jnp.int32, bucket.shape, 0)
        for h in range(N_HEADS):
            acc = jnp.full(bucket.shape, NEG_INF, F32)
            for bk in range(NUM_BUCKETS):
                acc = jnp.where(bucket == bk, table_ref[bk, h] * LOG2E, acc)
            g, hh = divmod(h, GROUP)
            c = slice(hh * BLK, (hh + 1) * BLK)
            bias_scr[0, g, :, c] = acc
            bias_scr[1, g, :, c] = jnp.where(row < BLK, NEG_INF, acc)
            bias_scr[2, g, :, c] = jnp.where(row >= 2 * BLK, NEG_INF, acc)

    k_all = jnp.concatenate([kp_ref[0], kc_ref[0], kn_ref[0]], axis=0)
    v_all = jnp.concatenate([vp_ref[0], vc_ref[0], vn_ref[0]], axis=0)
    vt_all = v_all.astype(F32).T.astype(BF16)
    for j in range(sub):
        if j == 0:
            variant = jnp.where(i == 0, 1, 0)
        elif j == sub - 1:
            variant = jnp.where(i == n_tiles - 1, 2, 0)
        else:
            variant = 0
        for g in range(N_KV_HEADS):
            heads = range(g * GROUP, (g + 1) * GROUP)
            q = jnp.concatenate(
                [q_ref[0, j * BLK:(j + 1) * BLK, h * HEAD_DIM:(h + 1) * HEAD_DIM] for h in heads],
                axis=0)
            k = k_all[j * BLK:(j + 3) * BLK, g * HEAD_DIM:(g + 1) * HEAD_DIM]
            vt = vt_all[g * HEAD_DIM:(g + 1) * HEAD_DIM, j * BLK:(j + 3) * BLK]
            s = lax.dot_general(k, q, (((1,), (1,)), ((), ())), preferred_element_type=F32)
            s = s + bias_scr[variant, g]
            sink = jnp.concatenate(
                [jnp.full((1, BLK), sink_ref[h] * LOG2E, F32) for h in heads], axis=1)
            m = jnp.maximum(jnp.max(s, axis=0, keepdims=True), sink)
            p = jnp.exp2(s - m)
            denom = jnp.sum(p, axis=0, keepdims=True) + jnp.exp2(sink - m)
            o = _dot(vt, p.astype(BF16)) * (1.0 / denom)
            for hh, h in enumerate(heads):
                o_ref[0, j * BLK:(j + 1) * BLK, h * HEAD_DIM:(h + 1) * HEAD_DIM] = (
                    o[:, hh * BLK:(hh + 1) * BLK].T.astype(o_ref.dtype))


def _attention(q, k, v, sink, rel_table):
    bsz, seq, q_w = q.shape
    kv_w = k.shape[2]
    assert seq % ATTN_TILE == 0 and seq >= 2 * BLK
    n_tiles = seq // ATTN_TILE
    sub = ATTN_TILE // BLK
    n_blk = seq // BLK
    smem = pl.BlockSpec(memory_space=pltpu.SMEM)
    cur = pl.BlockSpec((1, ATTN_TILE, kv_w), lambda b, i: (b, i, 0))
    prev = pl.BlockSpec((1, BLK, kv_w), lambda b, i: (b, jnp.maximum(i * sub - 1, 0), 0))
    nxt = pl.BlockSpec((1, BLK, kv_w), lambda b, i: (b, jnp.minimum((i + 1) * sub, n_blk - 1), 0))
    qo = pl.BlockSpec((1, ATTN_TILE, q_w), lambda b, i: (b, i, 0))
    bucket = jnp.asarray(_t5_bucket_table())
    return pl.pallas_call(
        functools.partial(_attn_body, n_tiles=n_tiles),
        out_shape=jax.ShapeDtypeStruct(q.shape, BF16),
        grid=(bsz, n_tiles),
        in_specs=[_resident(bucket.shape), smem, smem, qo, prev, cur, nxt, prev, cur, nxt],
        out_specs=qo,
        scratch_shapes=[pltpu.VMEM((3, N_KV_HEADS, 3 * BLK, GROUP * BLK), F32)],
        compiler_params=_params(2),
        name="attention",
    )(bucket, rel_table, sink, q, k, k, k, v, v, v)


def _rnn_body(x_ref, xp_ref, xn_ref, cw_ref, cb_ref, lam_ref, w_ref, ba_ref, bx_ref, *rest,
              reverse, n_tiles):
    if reverse:
        fwd_ref, o_ref, xc_scr, pa_scr, pi_scr, carry_scr = rest
    else:
        o_ref, xc_scr, pa_scr, pi_scr, carry_scr = rest
    t = pl.program_id(1)
    tt = (n_tiles - 1 - t) if reverse else t
    steps = x_ref.shape[1] // SUBLANES

    def by_step(v):
        return v.reshape(-1, SUBLANES, LANES)

    @pl.when(t == 0)
    def _():
        carry_scr[...] = jnp.zeros_like(carry_scr)

    x = by_step(x_ref[0])
    xp = by_step(jnp.where(tt == 0, 0.0, xp_ref[0]))
    xn = by_step(jnp.where(tt == n_tiles - 1, 0.0, xn_ref[0]))
    halo = xp.shape[0]
    ext = jnp.concatenate([xp, x, xn], axis=0)
    xc = cb_ref[...]
    for tap in range(CONV_W):
        off = halo + tap - CONV_LEFT
        xc = xc + ext[off:off + steps] * cw_ref[tap]
    xc_scr[...] = xc.reshape(steps * SUBLANES, LANES)

    for n in range(RNN_BLOCKS):
        pre = _dot(_block_rows(xc_scr, n, steps)[...].astype(BF16), w_ref[n])
        _block_rows(pa_scr, n, steps)[...] = pre[:, :RNN_BW]
        _block_rows(pi_scr, n, steps)[...] = pre[:, RNN_BW:]

    lam = lam_ref[...]
    neg = -lam
    softplus = jnp.maximum(neg, 0.0) + jnp.log1p(jnp.exp(-jnp.abs(neg)))
    k = (-0.5 * LRU_C * LOG2E) * softplus
    ta = jnp.tanh(by_step(pa_scr[...]) + 0.5 * ba_ref[...])
    a = jnp.exp2(k * ta + k)
    y = 1.0 - a * a
    gain = y * lax.rsqrt(jnp.maximum(y, 1e-37))
    ti = jnp.tanh(by_step(pi_scr[...]) + 0.5 * bx_ref[...])
    bb = gain * ((ti + 1.0) * (0.5 * xc))

    h = carry_scr[...]
    order = range(steps - 1, -1, -1) if reverse else range(steps)
    for s in order:
        h = a[s] * h + bb[s]
        r = slice(s * SUBLANES, (s + 1) * SUBLANES)
        o_ref[0, r, :] = (fwd_ref[0, r, :] + h) if reverse else h
    carry_scr[...] = h


def _rnn(xr, conv_w, conv_b, lam, w_gate, b_a, b_x, fwd=None):
    bsz, rows, _ = xr.shape
    reverse = fwd is not None
    tile_rows = RNN_TILE * SUBLANES
    halo_rows = SUBLANES * SUBLANES
    n_tiles = rows // tile_rows
    per = tile_rows // halo_rows

    def tidx(t):
        return (n_tiles - 1 - t) if reverse else t

    main = pl.BlockSpec((1, tile_rows, LANES), lambda b, t: (b, tidx(t), 0))
    prev = pl.BlockSpec((1, halo_rows, LANES),
                        lambda b, t: (b, jnp.maximum(tidx(t) * per - 1, 0), 0))
    nxt = pl.BlockSpec((1, halo_rows, LANES),
                       lambda b, t: (b, jnp.minimum((tidx(t) + 1) * per, n_tiles * per - 1), 0))
    in_specs = [main, prev, nxt, _resident(conv_w.shape), _resident(conv_b.shape),
                _resident(lam.shape), _resident(w_gate.shape), _resident(b_a.shape),
                _resident(b_x.shape)]
    args = [xr, xr, xr, conv_w, conv_b, lam, w_gate, b_a, b_x]
    if reverse:
        in_specs.append(main)
        args.append(fwd)
    return pl.pallas_call(
        functools.partial(_rnn_body, reverse=reverse, n_tiles=n_tiles),
        out_shape=jax.ShapeDtypeStruct(xr.shape, F32),
        grid=(bsz, n_tiles),
        in_specs=in_specs,
        out_specs=main,
        scratch_shapes=[pltpu.VMEM((tile_rows, LANES), F32)] * 3
        + [pltpu.VMEM((SUBLANES, LANES), F32)],
        compiler_params=_params(2),
        name="rnn_bwd" if reverse else "rnn_fwd",
    )(*args)


def _merge_body(x_ref, attn_ref, rec_ref, gy_ref, ga_ref, gr_ref, wa_ref, wr_ref, wo_ref, o_ref):
    rows = x_ref.shape[0]
    rnn = jnp.concatenate(
        [(_block_rows(rec_ref, blk, rows)[...] * gy_ref[:, blk * LANES:(blk + 1) * LANES]).astype(BF16)
         for blk in range(RNN_BLOCKS)], axis=1)
    ya = _dot(attn_ref[...], wa_ref[...])
    yr = _dot(rnn, wr_ref[...])
    merged = _sigmoid(ga_ref[...]) * ya + _sigmoid(gr_ref[...]) * yr
    o_ref[...] = x_ref[...] + _dot(merged.astype(BF16), wo_ref[...])


def _merge(x, attn, rec, gy, ga, gr, wa, wr, wo):
    n, d = x.shape

    def row(w):
        return pl.BlockSpec((ROW_TILE, w), lambda i: (i, 0))

    return pl.pallas_call(
        _merge_body,
        out_shape=jax.ShapeDtypeStruct((n, d), F32),
        grid=(n // ROW_TILE,),
        in_specs=[row(d), row(attn.shape[1]),
                  pl.BlockSpec((ROW_TILE * SUBLANES, LANES), lambda i: (i, 0)),
                  row(gy.shape[1]), row(d), row(d),
                  _resident(wa.shape), _resident(wr.shape), _resident(wo.shape)],
        out_specs=row(d),
        compiler_params=_params(1),
        name="merge",
    )(x, attn, rec, gy, ga, gr, wa, wr, wo)


def _trunk(x, p):
    bsz, seq, d = x.shape
    n = bsz * seq
    depth = p["w_in"].shape[0]
    d_rnn = RNN_BLOCKS * RNN_BW
    x = x.reshape(n, d)
    for l in range(depth):
        x = _ffn(x, p["ffn1_norm"][l], p["ffn1_wg"][l], p["ffn1_wu"][l], p["ffn1_wd"][l])
        q, k, v, xr, gy, ga, gr = _proj(x, p["mix_norm"][l], p["w_in"][l], d_rnn)
        attn = _attention(q.reshape(bsz, seq, -1), k.reshape(bsz, seq, -1),
                          v.reshape(bsz, seq, -1), p["attn_sink"][l], p["rel_bias_table"])
        rnn_args = (xr.reshape(bsz, seq * SUBLANES, LANES), p["conv_w"][l], p["conv_b"][l])
        fwd = _rnn(*rnn_args, p["lam"][l, 0], p["w_gate"][l, 0], p["b_a"][l, 0], p["b_x"][l, 0])
        rec = _rnn(*rnn_args, p["lam"][l, 1], p["w_gate"][l, 1], p["b_a"][l, 1], p["b_x"][l, 1],
                   fwd=fwd)
        x = _merge(x, attn.reshape(n, -1), rec.reshape(n * SUBLANES, LANES), gy, ga, gr,
                   p["w_br_attn"][l], p["w_br_rnn"][l], p["w_out"][l])
        x = _ffn(x, p["ffn2_norm"][l], p["ffn2_wg"][l], p["ffn2_wu"][l], p["ffn2_wd"][l],
                 final_g=p["final_norm"] if l == depth - 1 else None)
    return x.reshape(bsz, seq, d)


def kernel(x_prompt, x_sample, ffn1_norm, ffn1_w_up, ffn1_w_down, mix_norm, w_in, conv_w, conv_b,
           rg_lambda, rg_w_a, rg_b_a, rg_w_x, rg_b_x, attn_sink, rel_bias_table, w_br_attn,
           w_br_rnn, w_out, ffn2_norm, ffn2_w_up, ffn2_w_down, final_norm):
    depth, d = ffn1_norm.shape
    d_ff = ffn1_w_down.shape[1]
    blocked = (RNN_BLOCKS, RNN_BW)
    p = {
        "ffn1_norm": ffn1_norm.reshape(depth, 1, d),
        "ffn1_wg": ffn1_w_up[:, :, :d_ff].astype(BF16),
        "ffn1_wu": ffn1_w_up[:, :, d_ff:].astype(BF16),
        "ffn1_wd": ffn1_w_down.astype(BF16),
        "mix_norm": mix_norm.reshape(depth, 1, d),
        "w_in": w_in.astype(BF16),
        "conv_w": conv_w.reshape(depth, CONV_W, *blocked),
        "conv_b": conv_b.reshape(depth, *blocked),
        "lam": rg_lambda.reshape(depth, 2, *blocked),
        "w_gate": (0.5 * jnp.concatenate([rg_w_a, rg_w_x], axis=-1)).astype(BF16),
        "b_a": rg_b_a.reshape(depth, 2, *blocked),
        "b_x": rg_b_x.reshape(depth, 2, *blocked),
        "attn_sink": attn_sink,
        "rel_bias_table": rel_bias_table,
        "w_br_attn": w_br_attn.astype(BF16),
        "w_br_rnn": w_br_rnn.astype(BF16),
        "w_out": w_out.astype(BF16),
        "ffn2_norm": ffn2_norm.reshape(depth, 1, d),
        "ffn2_wg": ffn2_w_up[:, :, :d_ff].astype(BF16),
        "ffn2_wu": ffn2_w_up[:, :, d_ff:].astype(BF16),
        "ffn2_wd": ffn2_w_down.astype(BF16),
        "final_norm": final_norm.reshape(1, d),
    }
    return (_trunk(x_prompt, p), _trunk(x_sample, p))
```

```python
import functools
import math

import numpy as np
import jax
import jax.numpy as jnp
from jax import lax
from jax.experimental import pallas as pl
from jax.experimental.pallas import tpu as pltpu

F32 = jnp.float32
BF16 = jnp.bfloat16

N_HEADS = 8
HEAD_DIM = 128
N_KV_HEADS = 2
GROUP = N_HEADS // N_KV_HEADS
Q_W = N_HEADS * HEAD_DIM
KV_W = N_KV_HEADS * HEAD_DIM
WINDOW = 128
BLK = 128
NUM_BUCKETS = 32
MAX_DISTANCE = 128
RNN_BLOCKS = 8
RNN_BW = 128
D_RNN = RNN_BLOCKS * RNN_BW
CONV_W = 4
CONV_LEFT = 2
LRU_C = 8.0
EPS = 1e-6
NEG_INF = -1e30
LOG2E = math.log2(math.e)

SUBLANES = 8
LANES = 128
ROW_TILE = 512
RNN_SUB = 128
HALO = SUBLANES
FF_CHUNK = 512
VMEM_LIMIT = 58 * 1024 * 1024

assert RNN_BLOCKS == SUBLANES and RNN_BW == LANES and WINDOW == BLK
assert HALO >= max(CONV_LEFT, CONV_W - 1 - CONV_LEFT) and ROW_TILE % RNN_SUB == 0


def _params(n_axes):
    return pltpu.CompilerParams(
        dimension_semantics=("arbitrary",) * n_axes, vmem_limit_bytes=VMEM_LIMIT)


def _resident(shape):
    zeros = (0,) * len(shape)
    return pl.BlockSpec(shape, lambda *_: zeros, pipeline_mode=pl.Buffered(1))


_SMEM = pl.BlockSpec(memory_space=pltpu.SMEM)


def _rms(x, g):
    ms = jnp.mean(x * x, axis=-1, keepdims=True)
    return x * lax.rsqrt(ms + EPS) * g


def _sigmoid(x):
    return 0.5 * jnp.tanh(0.5 * x) + 0.5


def _gelu_tanh(x):
    return 0.5 * x * (1.0 + jnp.tanh(math.sqrt(2.0 / math.pi) * (x + 0.044715 * (x * x * x))))


def _dot(a, b):
    return jnp.dot(a, b, preferred_element_type=F32)


def _block_rows(ref, blk, n):
    return ref.at[pl.ds(blk, n, stride=SUBLANES), :]


def _by_step(v):
    return v.reshape(-1, SUBLANES, LANES)


def _ffn_body(x_ref, g_ref, wg_ref, wu_ref, wd_ref, *rest, final):
    if final:
        fg_ref, o_ref, a_scr = rest
    else:
        o_ref, a_scr = rest
    x = x_ref[...]
    h = _rms(x, g_ref[...]).astype(BF16)
    d_ff = wg_ref.shape[1]
    for s in range(0, d_ff, FF_CHUNK):
        e = min(s + FF_CHUNK, d_ff)
        gate = _dot(h, wg_ref[:, s:e])
        up = _dot(h, wu_ref[:, s:e])
        hg = 0.5 * gate
        a_scr[:, s:e] = ((hg + hg * jnp.tanh(hg)) * up).astype(BF16)
    y = x + 0.5 * _dot(a_scr[...], wd_ref[...])
    if final:
        y = _rms(y, fg_ref[...])
    o_ref[...] = y


def _ffn(x, g, wg, wu, wd, final_g=None):
    n, d = x.shape
    d_ff = wg.shape[1]
    final = final_g is not None
    row = pl.BlockSpec((ROW_TILE, d), lambda i: (i, 0))
    in_specs = [row, _resident((1, d)), _resident((d, d_ff)), _resident((d, d_ff)),
                _resident((d_ff, d))]
    args = [x, g, wg, wu, wd]
    if final:
        in_specs.append(_resident((1, d)))
        args.append(final_g)
    return pl.pallas_call(
        functools.partial(_ffn_body, final=final),
        out_shape=jax.ShapeDtypeStruct((n, d), F32),
        grid=(n // ROW_TILE,),
        in_specs=in_specs,
        out_specs=row,
        scratch_shapes=[pltpu.VMEM((ROW_TILE, d_ff), BF16)],
        compiler_params=_params(1),
        name="ffn_final" if final else "ffn",
    )(*args)


def _rnn_consts(lam_ref, ba_ref, bx_ref):
    neg = -lam_ref[...]
    softplus = jnp.maximum(neg, 0.0) + jnp.log1p(jnp.exp(-jnp.abs(neg)))
    return (-0.5 * LRU_C * LOG2E) * softplus, 0.5 * ba_ref[...], 0.5 * bx_ref[...]


def _rnn_steps(window, n, cw_ref, cb_ref, consts, w_ref, scratch, h, reverse, emit):
    xc_scr, pa_scr, pi_scr = scratch
    k, half_ba, half_bx = consts
    xc = cb_ref[...]
    for tap in range(CONV_W):
        xc = xc + window(tap - CONV_LEFT) * cw_ref[tap]
    xc_scr[...] = xc.reshape(n * SUBLANES, LANES)
    yield
    for blk in range(RNN_BLOCKS):
        pre = _dot(_block_rows(xc_scr, blk, n)[...].astype(BF16), w_ref[blk])
        _block_rows(pa_scr, blk, n)[...] = pre[:, :RNN_BW]
        _block_rows(pi_scr, blk, n)[...] = pre[:, RNN_BW:]
    yield
    ta = jnp.tanh(_by_step(pa_scr[...]) + half_ba)
    a = jnp.exp2(k * ta + k)
    y = 1.0 - a * a
    gain = y * lax.rsqrt(jnp.maximum(y, 1e-37))
    ti = jnp.tanh(_by_step(pi_scr[...]) + half_bx)
    bb = gain * ((ti + 1.0) * (0.5 * _by_step(xc_scr[...])))
    yield
    for s in (range(n - 1, -1, -1) if reverse else range(n)):
        h = a[s] * h + bb[s]
        emit(s, h)
    yield
    return h


def _interleave(*streams):
    live = list(streams)
    while live:
        for gen in list(live):
            try:
                next(gen)
            except StopIteration:
                live.remove(gen)


def _rnn_scratch(n_sub):
    return [pltpu.VMEM((RNN_SUB * SUBLANES, LANES), F32)] * (3 * n_sub)


def _t5_bucket_table():
    rel = np.arange(3 * BLK)[:, None] - BLK - np.arange(BLK)[None, :]
    n = NUM_BUCKETS // 2
    max_exact = n // 2
    ret = (rel > 0).astype(np.int32) * n
    na = np.abs(rel)
    large = max_exact + (np.log(np.maximum(na, 1) / max_exact)
                         / math.log(MAX_DISTANCE / max_exact) * (n - max_exact)).astype(np.int32)
    large = np.minimum(large, n - 1)
    bucket = ret + np.where(na < max_exact, na, large)
    return np.where(na <= WINDOW, bucket, -1).astype(np.int32)


def _build_bias(bucket_ref, table_ref, bias_scr):
    bucket = bucket_ref[...]
    row = lax.broadcasted_iota(jnp.int32, bucket.shape, 0)
    for h in range(N_HEADS):
        acc = jnp.full(bucket.shape, NEG_INF, F32)
        for bk in range(NUM_BUCKETS):
            acc = jnp.where(bucket == bk, table_ref[bk, h] * LOG2E, acc)
        g, hh = divmod(h, GROUP)
        c = slice(hh * BLK, (hh + 1) * BLK)
        bias_scr[0, g, :, c] = acc
        bias_scr[1, g, :, c] = jnp.where(row < BLK, NEG_INF, acc)
        bias_scr[2, g, :, c] = jnp.where(row >= 2 * BLK, NEG_INF, acc)


def _attend(q_ref, k_ref, v_ref, j, bias_scr, variant, sink_ref, o_ref):
    k_win = k_ref[j * BLK:(j + 3) * BLK, :]
    vt_win = v_ref[j * BLK:(j + 3) * BLK, :].astype(F32).T.astype(BF16)
    for g in range(N_KV_HEADS):
        heads = range(g * GROUP, (g + 1) * GROUP)
        q = jnp.concatenate(
            [q_ref[j * BLK:(j + 1) * BLK, h * HEAD_DIM:(h + 1) * HEAD_DIM] for h in heads], axis=0)
        k = k_win[:, g * HEAD_DIM:(g + 1) * HEAD_DIM]
        vt = vt_win[g * HEAD_DIM:(g + 1) * HEAD_DIM, :]
        s = lax.dot_general(k, q, (((1,), (1,)), ((), ())), preferred_element_type=F32)
        s = s + bias_scr[variant, g]
        sink = jnp.concatenate(
            [jnp.full((1, BLK), sink_ref[h] * LOG2E, F32) for h in heads], axis=1)
        m = jnp.maximum(jnp.max(s, axis=0, keepdims=True), sink)
        p = jnp.exp2(s - m)
        denom = jnp.sum(p, axis=0, keepdims=True) + jnp.exp2(sink - m)
        o = _dot(vt, p.astype(BF16)) * (1.0 / denom)
        for hh, h in enumerate(heads):
            o_ref[0, j * BLK:(j + 1) * BLK, h * HEAD_DIM:(h + 1) * HEAD_DIM] = (
                o[:, hh * BLK:(hh + 1) * BLK].T.astype(o_ref.dtype))
        yield


PROJ_CHUNK = 256


def _mixer_in_body(bucket_ref, table_ref, sink_ref, x_ref, g_ref, w_ref, cw_ref, cb_ref, lam_ref,
                   wgate_ref, ba_ref, bx_ref,
                   xr_ref, gy_ref, ga_ref, gr_ref, attn_ref, fwd_ref,
                   q_scr, k_scr, v_scr, e_scr, bias_scr, carry_scr, *rnn_scr, n_tiles):
    i = pl.program_id(0)
    tm = x_ref.shape[1]
    d = ga_ref.shape[2]
    halo_rows = HALO * SUBLANES
    main_rows = tm * SUBLANES
    cur_first = i % n_tiles == 0
    lag_first = i % n_tiles == 1
    past_end = cur_first
    new = i % 2
    old = 1 - new
    q_old, k_old, v_old, e_old = q_scr.at[old], k_scr.at[old], v_scr.at[old], e_scr.at[old]
    q_new, k_new, v_new, e_new = q_scr.at[new], k_scr.at[new], v_scr.at[new], e_scr.at[new]

    @pl.when(i == 0)
    def _():
        _build_bias(bucket_ref, table_ref, bias_scr)
        q_scr[...] = jnp.zeros_like(q_scr)
        k_scr[...] = jnp.zeros_like(k_scr)
        v_scr[...] = jnp.zeros_like(v_scr)
        e_scr[...] = jnp.zeros_like(e_scr)
        carry_scr[...] = jnp.zeros_like(carry_scr)

    h = _rms(x_ref[0], g_ref[...]).astype(BF16)

    def projection():
        col = {"q": 0, "k": Q_W, "v": Q_W + KV_W, "xr": Q_W + 2 * KV_W,
               "gy": Q_W + 2 * KV_W + D_RNN, "ga": Q_W + 2 * KV_W + 2 * D_RNN,
               "gr": Q_W + 2 * KV_W + 2 * D_RNN + d}
        for name, width in (("k", KV_W), ("v", KV_W), ("xr", D_RNN), ("q", Q_W),
                            ("gy", D_RNN), ("ga", d), ("gr", d)):
            chunk = min(PROJ_CHUNK, width)
            for c in range(0, width, chunk):
                y = _dot(h, w_ref[:, col[name] + c:col[name] + c + chunk])
                out = slice(c, c + chunk)
                if name == "k":
                    k_new[BLK:BLK + tm, out] = y.astype(BF16)
                    k_old[BLK + tm:, out] = y[:BLK].astype(BF16)
                elif name == "v":
                    v_new[BLK:BLK + tm, out] = y.astype(BF16)
                    v_old[BLK + tm:, out] = y[:BLK].astype(BF16)
                elif name == "xr":
                    for blk in range(c // LANES, (c + chunk) // LANES):
                        part = y[:, blk * LANES - c:(blk + 1) * LANES - c]
                        _block_rows(xr_ref.at[0], blk, tm)[...] = part
                        e_new[pl.ds(halo_rows + blk, tm, stride=SUBLANES), :] = part
                        e_old[pl.ds(halo_rows + main_rows + blk, HALO, stride=SUBLANES), :] = (
                            jnp.where(past_end, 0.0, part[:HALO]))
                elif name == "q":
                    q_new[:, out] = (y * (HEAD_DIM ** -0.5 * LOG2E)).astype(BF16)
                elif name == "gy":
                    gy_ref[0, :, out] = _gelu_tanh(y).astype(gy_ref.dtype)
                elif name == "ga":
                    ga_ref[0, :, out] = y.astype(ga_ref.dtype)
                else:
                    gr_ref[0, :, out] = y.astype(gr_ref.dtype)
                yield

    def lagged():
        consts = _rnn_consts(lam_ref, ba_ref, bx_ref)
        hstate = jnp.where(lag_first, 0.0, carry_scr[...])
        n_blk = tm // BLK
        for sub in range(tm // RNN_SUB):
            lo = HALO + sub * RNN_SUB

            def window(off, lo=lo):
                return _by_step(e_old[pl.ds((lo + off) * SUBLANES, RNN_SUB * SUBLANES), :])

            def emit(st, hv, sub=sub):
                fwd_ref[0, pl.ds((sub * RNN_SUB + st) * SUBLANES, SUBLANES), :] = hv

            hstate = yield from _rnn_steps(window, RNN_SUB, cw_ref, cb_ref, consts, wgate_ref,
                                           rnn_scr[3 * sub:3 * sub + 3], hstate, False, emit)
            for j in range(sub * RNN_SUB // BLK, (sub + 1) * RNN_SUB // BLK):
                if j == 0:
                    variant = jnp.where(lag_first, 1, 0)
                elif j == n_blk - 1:
                    variant = jnp.where(past_end, 2, 0)
                else:
                    variant = 0
                yield from _attend(q_old, k_old, v_old, j, bias_scr, variant, sink_ref, attn_ref)
        carry_scr[...] = hstate

    k_new[0:BLK, :] = k_old[tm:tm + BLK, :]
    v_new[0:BLK, :] = v_old[tm:tm + BLK, :]
    e_new[0:halo_rows, :] = jnp.where(cur_first, 0.0, e_old[main_rows:main_rows + halo_rows, :])

    _interleave(projection(), lagged())


def _mixer_in(x, p, l):
    bsz, seq, d = x.shape
    tm = ROW_TILE
    assert seq % tm == 0 and seq // tm >= 2
    n_tiles = seq // tm
    bucket = jnp.asarray(_t5_bucket_table())

    total = bsz * n_tiles

    def cur(i):
        flat = jnp.minimum(i, total - 1)
        return (flat // n_tiles, flat % n_tiles, 0)

    def lag(i):
        flat = jnp.maximum(i - 1, 0)
        return (flat // n_tiles, flat % n_tiles, 0)

    step_major = jax.ShapeDtypeStruct((bsz, seq * SUBLANES, LANES), F32)
    outs = [
        (step_major, pl.BlockSpec((1, tm * SUBLANES, LANES), cur)),
        (jax.ShapeDtypeStruct((bsz, seq, D_RNN), BF16), pl.BlockSpec((1, tm, D_RNN), cur)),
        (jax.ShapeDtypeStruct((bsz, seq, d), BF16), pl.BlockSpec((1, tm, d), cur)),
        (jax.ShapeDtypeStruct((bsz, seq, d), BF16), pl.BlockSpec((1, tm, d), cur)),
        (jax.ShapeDtypeStruct((bsz, seq, Q_W), BF16), pl.BlockSpec((1, tm, Q_W), lag)),
        (step_major, pl.BlockSpec((1, tm * SUBLANES, LANES), lag)),
    ]
    weights = [p["mix_norm"][l], p["w_in"][l], p["conv_w"][l], p["conv_b"][l], p["lam"][l, 0],
               p["w_gate"][l, 0], p["b_a"][l, 0], p["b_x"][l, 0]]
    return pl.pallas_call(
        functools.partial(_mixer_in_body, n_tiles=n_tiles),
        out_shape=[o[0] for o in outs],
        grid=(total + 1,),
        in_specs=[_resident(bucket.shape), _SMEM, _SMEM, pl.BlockSpec((1, tm, d), cur)]
        + [_resident(w.shape) for w in weights],
        out_specs=[o[1] for o in outs],
        scratch_shapes=[pltpu.VMEM((2, tm, Q_W), BF16),
                        pltpu.VMEM((2, tm + 2 * BLK, KV_W), BF16),
                        pltpu.VMEM((2, tm + 2 * BLK, KV_W), BF16),
                        pltpu.VMEM((2, (tm + 2 * HALO) * SUBLANES, LANES), F32),
                        pltpu.VMEM((3, N_KV_HEADS, 3 * BLK, GROUP * BLK), F32),
                        pltpu.VMEM((SUBLANES, LANES), F32)] + _rnn_scratch(tm // RNN_SUB),
        compiler_params=_params(1),
        name="mixer_in",
    )(bucket, p["rel_bias_table"], p["attn_sink"][l], x, *weights)


def _mixer_out_body(x_ref, attn_ref, xr_ref, xp_ref, xn_ref, fwd_ref, gy_ref, ga_ref, gr_ref,
                    cw_ref, cb_ref, lam_ref, wgate_ref, ba_ref, bx_ref, wa_ref, wr_ref, wo_ref,
                    o_ref, rec_scr, ya_scr, carry_scr, *rnn_scr, n_tiles):
    t = pl.program_id(1)
    tt = n_tiles - 1 - t
    tm = x_ref.shape[1]

    @pl.when(t == 0)
    def _():
        carry_scr[...] = jnp.zeros_like(carry_scr)

    def attn_branch():
        for c in range(0, wa_ref.shape[1], PROJ_CHUNK):
            ya_scr[:, c:c + PROJ_CHUNK] = _dot(attn_ref[0], wa_ref[:, c:c + PROJ_CHUNK])
            yield

    def scan():
        ext = jnp.concatenate(
            [_by_step(jnp.where(tt == 0, 0.0, xp_ref[0])), _by_step(xr_ref[0]),
             _by_step(jnp.where(tt == n_tiles - 1, 0.0, xn_ref[0]))], axis=0)
        consts = _rnn_consts(lam_ref, ba_ref, bx_ref)
        hstate = carry_scr[...]
        for sub in range(tm // RNN_SUB - 1, -1, -1):
            lo = HALO + sub * RNN_SUB

            def window(off, lo=lo):
                return ext[lo + off:lo + off + RNN_SUB]

            def emit(st, hv, sub=sub):
                r = pl.ds((sub * RNN_SUB + st) * SUBLANES, SUBLANES)
                rec_scr[r, :] = fwd_ref[0, r, :] + hv

            hstate = yield from _rnn_steps(window, RNN_SUB, cw_ref, cb_ref, consts, wgate_ref,
                                           rnn_scr[3 * sub:3 * sub + 3], hstate, True, emit)
        carry_scr[...] = hstate

    _interleave(scan(), attn_branch())

    rnn = jnp.concatenate(
        [(_block_rows(rec_scr, blk, tm)[...]
          * gy_ref[0, :, blk * LANES:(blk + 1) * LANES].astype(F32)).astype(BF16)
         for blk in range(RNN_BLOCKS)], axis=1)
    yr = _dot(rnn, wr_ref[...])
    merged = (_sigmoid(ga_ref[0].astype(F32)) * ya_scr[...]
              + _sigmoid(gr_ref[0].astype(F32)) * yr)
    o_ref[0] = x_ref[0] + _dot(merged.astype(BF16), wo_ref[...])


def _mixer_out(x, attn, xr, fwd, gy, ga, gr, p, l):
    bsz, seq, d = x.shape
    tm = ROW_TILE
    n_tiles = seq // tm
    per = tm // HALO
    halo_rows = HALO * SUBLANES

    def tile(b, t):
        return (b, n_tiles - 1 - t, 0)

    def row(w):
        return pl.BlockSpec((1, tm, w), tile)

    main = pl.BlockSpec((1, tm * SUBLANES, LANES), tile)
    prev = pl.BlockSpec((1, halo_rows, LANES),
                        lambda b, t: (b, jnp.maximum((n_tiles - 1 - t) * per - 1, 0), 0))
    nxt = pl.BlockSpec((1, halo_rows, LANES),
                       lambda b, t: (b, jnp.minimum((n_tiles - t) * per, n_tiles * per - 1), 0))
    weights = [p["conv_w"][l], p["conv_b"][l], p["lam"][l, 1], p["w_gate"][l, 1], p["b_a"][l, 1],
               p["b_x"][l, 1], p["w_br_attn"][l], p["w_br_rnn"][l], p["w_out"][l]]
    return pl.pallas_call(
        functools.partial(_mixer_out_body, n_tiles=n_tiles),
        out_shape=jax.ShapeDtypeStruct(x.shape, F32),
        grid=(bsz, n_tiles),
        in_specs=[row(d), row(Q_W), main, prev, nxt, main, row(D_RNN), row(d), row(d)]
        + [_resident(w.shape) for w in weights],
        out_specs=row(d),
        scratch_shapes=[pltpu.VMEM((tm * SUBLANES, LANES), F32), pltpu.VMEM((tm, d), F32),
                        pltpu.VMEM((SUBLANES, LANES), F32)] + _rnn_scratch(tm // RNN_SUB),
        compiler_params=_params(2),
        name="mixer_out",
    )(x, attn, xr, xr, xr, fwd, gy, ga, gr, *weights)


def _trunk(x, p):
    bsz, seq, d = x.shape
    depth = p["w_in"].shape[0]

    def ffn(x, which, l, final_g=None):
        y = _ffn(x.reshape(bsz * seq, d), p[which + "_norm"][l], p[which + "_wg"][l],
                 p[which + "_wu"][l], p[which + "_wd"][l], final_g=final_g)
        return y.reshape(bsz, seq, d)

    for l in range(depth):
        x = ffn(x, "ffn1", l)
        xr, gy, ga, gr, attn, fwd = _mixer_in(x, p, l)
        x = _mixer_out(x, attn, xr, fwd, gy, ga, gr, p, l)
        x = ffn(x, "ffn2", l, final_g=p["final_norm"] if l == depth - 1 else None)
    return x


def kernel(x_prompt, x_sample, ffn1_norm, ffn1_w_up, ffn1_w_down, mix_norm, w_in, conv_w, conv_b,
           rg_lambda, rg_w_a, rg_b_a, rg_w_x, rg_b_x, attn_sink, rel_bias_table, w_br_attn,
           w_br_rnn, w_out, ffn2_norm, ffn2_w_up, ffn2_w_down, final_norm):
    depth, d = ffn1_norm.shape
    d_ff = ffn1_w_down.shape[1]
    assert conv_w.shape[2] == D_RNN and w_in.shape[2] == Q_W + 2 * KV_W + 2 * D_RNN + 2 * d
    blocked = (RNN_BLOCKS, RNN_BW)
    p = {
        "ffn1_norm": ffn1_norm.reshape(depth, 1, d),
        "ffn1_wg": ffn1_w_up[:, :, :d_ff].astype(BF16),
        "ffn1_wu": ffn1_w_up[:, :, d_ff:].astype(BF16),
        "ffn1_wd": ffn1_w_down.astype(BF16),
        "mix_norm": mix_norm.reshape(depth, 1, d),
        "w_in": w_in.astype(BF16),
        "conv_w": conv_w.reshape(depth, CONV_W, *blocked),
        "conv_b": conv_b.reshape(depth, *blocked),
        "lam": rg_lambda.reshape(depth, 2, *blocked),
        "w_gate": (0.5 * jnp.concatenate([rg_w_a, rg_w_x], axis=-1)).astype(BF16),
        "b_a": rg_b_a.reshape(depth, 2, *blocked),
        "b_x": rg_b_x.reshape(depth, 2, *blocked),
        "attn_sink": attn_sink,
        "rel_bias_table": rel_bias_table,
        "w_br_attn": w_br_attn.astype(BF16),
        "w_br_rnn": w_br_rnn.astype(BF16),
        "w_out": w_out.astype(BF16),
        "ffn2_norm": ffn2_norm.reshape(depth, 1, d),
        "ffn2_wg": ffn2_w_up[:, :, :d_ff].astype(BF16),
        "ffn2_wu": ffn2_w_up[:, :, d_ff:].astype(BF16),
        "ffn2_wd": ffn2_w_down.astype(BF16),
        "final_norm": final_norm.reshape(1, d),
    }
    return (_trunk(x_prompt, p), _trunk(x_sample, p))
```

```python
import functools
import math

import numpy as np
import jax
import jax.numpy as jnp
from jax import lax
from jax.experimental import pallas as pl
from jax.experimental.pallas import tpu as pltpu

F32 = jnp.float32
BF16 = jnp.bfloat16

N_HEADS = 8
HEAD_DIM = 128
N_KV_HEADS = 2
GROUP = N_HEADS // N_KV_HEADS
Q_W = N_HEADS * HEAD_DIM
KV_W = N_KV_HEADS * HEAD_DIM
WINDOW = 128
BLK = 128
NUM_BUCKETS = 32
MAX_DISTANCE = 128
RNN_BLOCKS = 8
RNN_BW = 128
D_RNN = RNN_BLOCKS * RNN_BW
CONV_W = 4
CONV_LEFT = 2
LRU_C = 8.0
EPS = 1e-6
NEG_INF = -1e30
LOG2E = math.log2(math.e)

SUBLANES = 8
LANES = 128
ROW_TILE = 512
FFN_TILE = 1024
RNN_SUB = 128
HALO = SUBLANES
FF_CHUNK = 512
VMEM_LIMIT = 58 * 1024 * 1024

assert RNN_BLOCKS == SUBLANES and RNN_BW == LANES and WINDOW == BLK
assert HALO >= max(CONV_LEFT, CONV_W - 1 - CONV_LEFT) and ROW_TILE % RNN_SUB == 0


def _params(n_axes):
    return pltpu.CompilerParams(
        dimension_semantics=("arbitrary",) * n_axes, vmem_limit_bytes=VMEM_LIMIT)


def _resident(shape):
    zeros = (0,) * len(shape)
    return pl.BlockSpec(shape, lambda *_: zeros, pipeline_mode=pl.Buffered(1))


_SMEM = pl.BlockSpec(memory_space=pltpu.SMEM)


def _rms(x, g):
    ms = jnp.mean(x * x, axis=-1, keepdims=True)
    return x * lax.rsqrt(ms + EPS) * g


def _sigmoid(x):
    return 0.5 * jnp.tanh(0.5 * x) + 0.5


def _gelu_tanh(x):
    return 0.5 * x * (1.0 + jnp.tanh(math.sqrt(2.0 / math.pi) * (x + 0.044715 * (x * x * x))))


def _dot(a, b):
    return jnp.dot(a, b, preferred_element_type=F32)


def _block_rows(ref, blk, n):
    return ref.at[pl.ds(blk, n, stride=SUBLANES), :]


def _by_step(v):
    return v.reshape(-1, SUBLANES, LANES)


def _ffn_body(x_ref, g_ref, wg_ref, wu_ref, wd_ref, *rest, final):
    if final:
        fg_ref, o_ref, a_scr = rest
    else:
        o_ref, a_scr = rest
    x = x_ref[...]
    h = _rms(x, g_ref[...]).astype(BF16)
    d_ff = wg_ref.shape[1]
    for s in range(0, d_ff, FF_CHUNK):
        e = min(s + FF_CHUNK, d_ff)
        gate = _dot(h, wg_ref[:, s:e])
        up = _dot(h, wu_ref[:, s:e])
        hg = 0.5 * gate
        a_scr[:, s:e] = ((hg + hg * jnp.tanh(hg)) * up).astype(BF16)
    y = x + 0.5 * _dot(a_scr[...], wd_ref[...])
    if final:
        y = _rms(y, fg_ref[...])
    o_ref[...] = y


def _ffn(x, g, wg, wu, wd, final_g=None):
    n, d = x.shape
    d_ff = wg.shape[1]
    final = final_g is not None
    row = pl.BlockSpec((FFN_TILE, d), lambda i: (i, 0))
    in_specs = [row, _resident((1, d)), _resident((d, d_ff)), _resident((d, d_ff)),
                _resident((d_ff, d))]
    args = [x, g, wg, wu, wd]
    if final:
        in_specs.append(_resident((1, d)))
        args.append(final_g)
    return pl.pallas_call(
        functools.partial(_ffn_body, final=final),
        out_shape=jax.ShapeDtypeStruct((n, d), F32),
        grid=(n // FFN_TILE,),
        in_specs=in_specs,
        out_specs=row,
        scratch_shapes=[pltpu.VMEM((FFN_TILE, d_ff), BF16)],
        compiler_params=_params(1),
        name="ffn_final" if final else "ffn",
    )(*args)


def _rnn_consts(lam_ref, ba_ref, bx_ref):
    neg = -lam_ref[...]
    softplus = jnp.maximum(neg, 0.0) + jnp.log1p(jnp.exp(-jnp.abs(neg)))
    return (-0.5 * LRU_C * LOG2E) * softplus, 0.5 * ba_ref[...], 0.5 * bx_ref[...]


def _conv(window, cw_ref, cb_ref):
    xc = cb_ref[...]
    for tap in range(CONV_W):
        xc = xc + window(tap - CONV_LEFT) * cw_ref[tap]
    return xc


def _rnn_steps(xc_ref, base, n, consts, w_ref, scratch, h, reverse, emit):
    pa_scr, pi_scr = scratch
    k, half_ba, half_bx = consts
    for blk in range(RNN_BLOCKS):
        pre = _dot(_block_rows(xc_ref, base + blk, n)[...].astype(BF16), w_ref[blk])
        _block_rows(pa_scr, blk, n)[...] = pre[:, :RNN_BW]
        _block_rows(pi_scr, blk, n)[...] = pre[:, RNN_BW:]
    yield
    ta = jnp.tanh(_by_step(pa_scr[...]) + half_ba)
    a = jnp.exp2(k * ta + k)
    y = 1.0 - a * a
    gain = y * lax.rsqrt(jnp.maximum(y, 1e-37))
    ti = jnp.tanh(_by_step(pi_scr[...]) + half_bx)
    xc = _by_step(xc_ref[pl.ds(base, n * SUBLANES), :])
    bb = gain * ((ti + 1.0) * (0.5 * xc))
    yield
    for s in (range(n - 1, -1, -1) if reverse else range(n)):
        h = a[s] * h + bb[s]
        emit(s, h)
    yield
    return h


def _interleave(*streams):
    live = list(streams)
    while live:
        for gen in list(live):
            try:
                next(gen)
            except StopIteration:
                live.remove(gen)


def _rnn_scratch(n_sub):
    return [pltpu.VMEM((RNN_SUB * SUBLANES, LANES), F32)] * (2 * n_sub)


def _t5_bucket_table():
    rel = np.arange(3 * BLK)[:, None] - BLK - np.arange(BLK)[None, :]
    n = NUM_BUCKETS // 2
    max_exact = n // 2
    ret = (rel > 0).astype(np.int32) * n
    na = np.abs(rel)
    large = max_exact + (np.log(np.maximum(na, 1) / max_exact)
                         / math.log(MAX_DISTANCE / max_exact) * (n - max_exact)).astype(np.int32)
    large = np.minimum(large, n - 1)
    bucket = ret + np.where(na < max_exact, na, large)
    return np.where(na <= WINDOW, bucket, -1).astype(np.int32)


def _build_bias(bucket_ref, table_ref, bias_scr):
    bucket = bucket_ref[...]
    row = lax.broadcasted_iota(jnp.int32, bucket.shape, 0)
    for h in range(N_HEADS):
        acc = jnp.full(bucket.shape, NEG_INF, F32)
        for bk in range(NUM_BUCKETS):
            acc = jnp.where(bucket == bk, table_ref[bk, h] * LOG2E, acc)
        g, hh = divmod(h, GROUP)
        c = slice(hh * BLK, (hh + 1) * BLK)
        bias_scr[0, g, :, c] = acc
        bias_scr[1, g, :, c] = jnp.where(row < BLK, NEG_INF, acc)
        bias_scr[2, g, :, c] = jnp.where(row >= 2 * BLK, NEG_INF, acc)


def _attend(q_ref, k_ref, v_ref, j, bias_scr, variant, sink_ref, o_ref):
    k_win = k_ref[j * BLK:(j + 3) * BLK, :]
    vt_win = v_ref[j * BLK:(j + 3) * BLK, :].astype(F32).T.astype(BF16)
    for g in range(N_KV_HEADS):
        heads = range(g * GROUP, (g + 1) * GROUP)
        q = jnp.concatenate(
            [q_ref[j * BLK:(j + 1) * BLK, h * HEAD_DIM:(h + 1) * HEAD_DIM] for h in heads], axis=0)
        k = k_win[:, g * HEAD_DIM:(g + 1) * HEAD_DIM]
        vt = vt_win[g * HEAD_DIM:(g + 1) * HEAD_DIM, :]
        s = lax.dot_general(k, q, (((1,), (1,)), ((), ())), preferred_element_type=F32)
        s = s + bias_scr[variant, g]
        sink = jnp.concatenate(
            [jnp.full((1, BLK), sink_ref[h] * LOG2E, F32) for h in heads], axis=1)
        m = jnp.maximum(jnp.max(s, axis=0, keepdims=True), sink)
        p = jnp.exp2(s - m)
        denom = jnp.sum(p, axis=0, keepdims=True) + jnp.exp2(sink - m)
        o = _dot(vt, p.astype(BF16)) * (1.0 / denom)
        for hh, h in enumerate(heads):
            o_ref[0, j * BLK:(j + 1) * BLK, h * HEAD_DIM:(h + 1) * HEAD_DIM] = (
                o[:, hh * BLK:(hh + 1) * BLK].T.astype(o_ref.dtype))
        yield


PROJ_CHUNK = 256


def _mixer_in_body(bucket_ref, table_ref, sink_ref, x_ref, g_ref, w_ref, cw_ref, cb_ref, lam_ref,
                   wgate_ref, ba_ref, bx_ref,
                   gy_ref, ga_ref, gr_ref, attn_ref, xc_ref, fwd_ref,
                   q_scr, k_scr, v_scr, e_scr, bias_scr, carry_scr, *rnn_scr, n_tiles):
    i = pl.program_id(0)
    tm = x_ref.shape[1]
    d = ga_ref.shape[2]
    halo_rows = HALO * SUBLANES
    main_rows = tm * SUBLANES
    cur_first = i % n_tiles == 0
    lag_first = i % n_tiles == 1
    past_end = cur_first
    new = i % 2
    old = 1 - new
    q_old, k_old, v_old, e_old = q_scr.at[old], k_scr.at[old], v_scr.at[old], e_scr.at[old]
    q_new, k_new, v_new, e_new = q_scr.at[new], k_scr.at[new], v_scr.at[new], e_scr.at[new]

    @pl.when(i == 0)
    def _():
        _build_bias(bucket_ref, table_ref, bias_scr)
        q_scr[...] = jnp.zeros_like(q_scr)
        k_scr[...] = jnp.zeros_like(k_scr)
        v_scr[...] = jnp.zeros_like(v_scr)
        e_scr[...] = jnp.zeros_like(e_scr)
        carry_scr[...] = jnp.zeros_like(carry_scr)

    h = _rms(x_ref[0], g_ref[...]).astype(BF16)

    def projection():
        col = {"q": 0, "k": Q_W, "v": Q_W + KV_W, "xr": Q_W + 2 * KV_W,
               "gy": Q_W + 2 * KV_W + D_RNN, "ga": Q_W + 2 * KV_W + 2 * D_RNN,
               "gr": Q_W + 2 * KV_W + 2 * D_RNN + d}
        for name, width in (("k", KV_W), ("v", KV_W), ("xr", D_RNN), ("q", Q_W),
                            ("gy", D_RNN), ("ga", d), ("gr", d)):
            chunk = min(PROJ_CHUNK, width)
            for c in range(0, width, chunk):
                y = _dot(h, w_ref[:, col[name] + c:col[name] + c + chunk])
                out = slice(c, c + chunk)
                if name == "k":
                    k_new[BLK:BLK + tm, out] = y.astype(BF16)
                    k_old[BLK + tm:, out] = y[:BLK].astype(BF16)
                elif name == "v":
                    v_new[BLK:BLK + tm, out] = y.astype(BF16)
                    v_old[BLK + tm:, out] = y[:BLK].astype(BF16)
                elif name == "xr":
                    for blk in range(c // LANES, (c + chunk) // LANES):
                        part = y[:, blk * LANES - c:(blk + 1) * LANES - c]
                        e_new[pl.ds(halo_rows + blk, tm, stride=SUBLANES), :] = part
                        e_old[pl.ds(halo_rows + main_rows + blk, HALO, stride=SUBLANES), :] = (
                            jnp.where(past_end, 0.0, part[:HALO]))
                elif name == "q":
                    q_new[:, out] = (y * (HEAD_DIM ** -0.5 * LOG2E)).astype(BF16)
                elif name == "gy":
                    gy_ref[0, :, out] = _gelu_tanh(y).astype(gy_ref.dtype)
                elif name == "ga":
                    ga_ref[0, :, out] = y.astype(ga_ref.dtype)
                else:
                    gr_ref[0, :, out] = y.astype(gr_ref.dtype)
                yield

    def lagged():
        consts = _rnn_consts(lam_ref, ba_ref, bx_ref)
        hstate = jnp.where(lag_first, 0.0, carry_scr[...])
        n_blk = tm // BLK
        for sub in range(tm // RNN_SUB):
            lo = HALO + sub * RNN_SUB

            def window(off, lo=lo):
                return _by_step(e_old[pl.ds((lo + off) * SUBLANES, RNN_SUB * SUBLANES), :])

            def emit(st, hv, sub=sub):
                fwd_ref[0, pl.ds((sub * RNN_SUB + st) * SUBLANES, SUBLANES), :] = hv

            base = sub * RNN_SUB * SUBLANES
            xc_ref[0, pl.ds(base, RNN_SUB * SUBLANES), :] = (
                _conv(window, cw_ref, cb_ref).reshape(RNN_SUB * SUBLANES, LANES))
            yield
            hstate = yield from _rnn_steps(xc_ref.at[0], base, RNN_SUB, consts, wgate_ref,
                                           rnn_scr[2 * sub:2 * sub + 2], hstate, False, emit)
            for j in range(sub * RNN_SUB // BLK, (sub + 1) * RNN_SUB // BLK):
                if j == 0:
                    variant = jnp.where(lag_first, 1, 0)
                elif j == n_blk - 1:
                    variant = jnp.where(past_end, 2, 0)
                else:
                    variant = 0
                yield from _attend(q_old, k_old, v_old, j, bias_scr, variant, sink_ref, attn_ref)
        carry_scr[...] = hstate

    k_new[0:BLK, :] = k_old[tm:tm + BLK, :]
    v_new[0:BLK, :] = v_old[tm:tm + BLK, :]
    e_new[0:halo_rows, :] = jnp.where(cur_first, 0.0, e_old[main_rows:main_rows + halo_rows, :])

    _interleave(projection(), lagged())


def _mixer_in(x, p, l):
    bsz, seq, d = x.shape
    tm = ROW_TILE
    assert seq % tm == 0 and seq // tm >= 2
    n_tiles = seq // tm
    bucket = jnp.asarray(_t5_bucket_table())

    total = bsz * n_tiles

    def cur(i):
        flat = jnp.minimum(i, total - 1)
        return (flat // n_tiles, flat % n_tiles, 0)

    def lag(i):
        flat = jnp.maximum(i - 1, 0)
        return (flat // n_tiles, flat % n_tiles, 0)

    step_major = jax.ShapeDtypeStruct((bsz, seq * SUBLANES, LANES), F32)
    outs = [
        (jax.ShapeDtypeStruct((bsz, seq, D_RNN), BF16), pl.BlockSpec((1, tm, D_RNN), cur)),
        (jax.ShapeDtypeStruct((bsz, seq, d), BF16), pl.BlockSpec((1, tm, d), cur)),
        (jax.ShapeDtypeStruct((bsz, seq, d), BF16), pl.BlockSpec((1, tm, d), cur)),
        (jax.ShapeDtypeStruct((bsz, seq, Q_W), BF16), pl.BlockSpec((1, tm, Q_W), lag)),
        (step_major, pl.BlockSpec((1, tm * SUBLANES, LANES), lag)),
        (step_major, pl.BlockSpec((1, tm * SUBLANES, LANES), lag)),
    ]
    weights = [p["mix_norm"][l], p["w_in"][l], p["conv_w"][l], p["conv_b"][l], p["lam"][l, 0],
               p["w_gate"][l, 0], p["b_a"][l, 0], p["b_x"][l, 0]]
    return pl.pallas_call(
        functools.partial(_mixer_in_body, n_tiles=n_tiles),
        out_shape=[o[0] for o in outs],
        grid=(total + 1,),
        in_specs=[_resident(bucket.shape), _SMEM, _SMEM, pl.BlockSpec((1, tm, d), cur)]
        + [_resident(w.shape) for w in weights],
        out_specs=[o[1] for o in outs],
        scratch_shapes=[pltpu.VMEM((2, tm, Q_W), BF16),
                        pltpu.VMEM((2, tm + 2 * BLK, KV_W), BF16),
                        pltpu.VMEM((2, tm + 2 * BLK, KV_W), BF16),
                        pltpu.VMEM((2, (tm + 2 * HALO) * SUBLANES, LANES), F32),
                        pltpu.VMEM((3, N_KV_HEADS, 3 * BLK, GROUP * BLK), F32),
                        pltpu.VMEM((SUBLANES, LANES), F32)] + _rnn_scratch(tm // RNN_SUB),
        compiler_params=_params(1),
        name="mixer_in",
    )(bucket, p["rel_bias_table"], p["attn_sink"][l], x, *weights)


def _mixer_out_body(x_ref, attn_ref, xc_ref, fwd_ref, gy_ref, ga_ref, gr_ref,
                    lam_ref, wgate_ref, ba_ref, bx_ref, wa_ref, wr_ref, wo_ref,
                    o_ref, rec_scr, ya_scr, carry_scr, *rnn_scr):
    t = pl.program_id(1)
    tm = x_ref.shape[1]

    @pl.when(t == 0)
    def _():
        carry_scr[...] = jnp.zeros_like(carry_scr)

    def attn_branch():
        for c in range(0, wa_ref.shape[1], PROJ_CHUNK):
            ya_scr[:, c:c + PROJ_CHUNK] = _dot(attn_ref[0], wa_ref[:, c:c + PROJ_CHUNK])
            yield

    def scan():
        consts = _rnn_consts(lam_ref, ba_ref, bx_ref)
        hstate = carry_scr[...]
        for sub in range(tm // RNN_SUB - 1, -1, -1):
            base = sub * RNN_SUB * SUBLANES

            def emit(st, hv, base=base):
                r = pl.ds(base + st * SUBLANES, SUBLANES)
                rec_scr[r, :] = fwd_ref[0, r, :] + hv

            hstate = yield from _rnn_steps(xc_ref.at[0], base, RNN_SUB, consts, wgate_ref,
                                           rnn_scr[2 * sub:2 * sub + 2], hstate, True, emit)
        carry_scr[...] = hstate

    _interleave(scan(), attn_branch())

    rnn = jnp.concatenate(
        [(_block_rows(rec_scr, blk, tm)[...]
          * gy_ref[0, :, blk * LANES:(blk + 1) * LANES].astype(F32)).astype(BF16)
         for blk in range(RNN_BLOCKS)], axis=1)
    yr = _dot(rnn, wr_ref[...])
    merged = (_sigmoid(ga_ref[0].astype(F32)) * ya_scr[...]
              + _sigmoid(gr_ref[0].astype(F32)) * yr)
    o_ref[0] = x_ref[0] + _dot(merged.astype(BF16), wo_ref[...])


def _mixer_out(x, attn, xc, fwd, gy, ga, gr, p, l):
    bsz, seq, d = x.shape
    tm = ROW_TILE
    n_tiles = seq // tm

    def tile(b, t):
        return (b, n_tiles - 1 - t, 0)

    def row(w):
        return pl.BlockSpec((1, tm, w), tile)

    main = pl.BlockSpec((1, tm * SUBLANES, LANES), tile)
    weights = [p["lam"][l, 1], p["w_gate"][l, 1], p["b_a"][l, 1], p["b_x"][l, 1],
               p["w_br_attn"][l], p["w_br_rnn"][l], p["w_out"][l]]
    return pl.pallas_call(
        _mixer_out_body,
        out_shape=jax.ShapeDtypeStruct(x.shape, F32),
        grid=(bsz, n_tiles),
        in_specs=[row(d), row(Q_W), main, main, row(D_RNN), row(d), row(d)]
        + [_resident(w.shape) for w in weights],
        out_specs=row(d),
        scratch_shapes=[pltpu.VMEM((tm * SUBLANES, LANES), F32), pltpu.VMEM((tm, d), F32),
                        pltpu.VMEM((SUBLANES, LANES), F32)] + _rnn_scratch(tm // RNN_SUB),
        compiler_params=_params(2),
        name="mixer_out",
    )(x, attn, xc, fwd, gy, ga, gr, *weights)


def _trunk(x, p):
    bsz, seq, d = x.shape
    depth = p["w_in"].shape[0]

    def ffn(x, which, l, final_g=None):
        y = _ffn(x.reshape(bsz * seq, d), p[which + "_norm"][l], p[which + "_wg"][l],
                 p[which + "_wu"][l], p[which + "_wd"][l], final_g=final_g)
        return y.reshape(bsz, seq, d)

    for l in range(depth):
        x = ffn(x, "ffn1", l)
        gy, ga, gr, attn, xc, fwd = _mixer_in(x, p, l)
        x = _mixer_out(x, attn, xc, fwd, gy, ga, gr, p, l)
        x = ffn(x, "ffn2", l, final_g=p["final_norm"] if l == depth - 1 else None)
    return x


def kernel(x_prompt, x_sample, ffn1_norm, ffn1_w_up, ffn1_w_down, mix_norm, w_in, conv_w, conv_b,
           rg_lambda, rg_w_a, rg_b_a, rg_w_x, rg_b_x, attn_sink, rel_bias_table, w_br_attn,
           w_br_rnn, w_out, ffn2_norm, ffn2_w_up, ffn2_w_down, final_norm):
    depth, d = ffn1_norm.shape
    d_ff = ffn1_w_down.shape[1]
    assert conv_w.shape[2] == D_RNN and w_in.shape[2] == Q_W + 2 * KV_W + 2 * D_RNN + 2 * d
    blocked = (RNN_BLOCKS, RNN_BW)
    p = {
        "ffn1_norm": ffn1_norm.reshape(depth, 1, d),
        "ffn1_wg": ffn1_w_up[:, :, :d_ff].astype(BF16),
        "ffn1_wu": ffn1_w_up[:, :, d_ff:].astype(BF16),
        "ffn1_wd": ffn1_w_down.astype(BF16),
        "mix_norm": mix_norm.reshape(depth, 1, d),
        "w_in": w_in.astype(BF16),
        "conv_w": conv_w.reshape(depth, CONV_W, *blocked),
        "conv_b": conv_b.reshape(depth, *blocked),
        "lam": rg_lambda.reshape(depth, 2, *blocked),
        "w_gate": (0.5 * jnp.concatenate([rg_w_a, rg_w_x], axis=-1)).astype(BF16),
        "b_a": rg_b_a.reshape(depth, 2, *blocked),
        "b_x": rg_b_x.reshape(depth, 2, *blocked),
        "attn_sink": attn_sink,
        "rel_bias_table": rel_bias_table,
        "w_br_attn": w_br_attn.astype(BF16),
        "w_br_rnn": w_br_rnn.astype(BF16),
        "w_out": w_out.astype(BF16),
        "ffn2_norm": ffn2_norm.reshape(depth, 1, d),
        "ffn2_wg": ffn2_w_up[:, :, :d_ff].astype(BF16),
        "ffn2_wu": ffn2_w_up[:, :, d_ff:].astype(BF16),
        "ffn2_wd": ffn2_w_down.astype(BF16),
        "final_norm": final_norm.reshape(1, d),
    }
    return (_trunk(x_prompt, p), _trunk(x_sample, p))
```

```python
import functools
import math

import numpy as np
import jax
import jax.numpy as jnp
from jax import lax
from jax.experimental import pallas as pl
from jax.experimental.pallas import tpu as pltpu

F32 = jnp.float32
BF16 = jnp.bfloat16

N_HEADS = 8
HEAD_DIM = 128
N_KV_HEADS = 2
GROUP = N_HEADS // N_KV_HEADS
Q_W = N_HEADS * HEAD_DIM
KV_W = N_KV_HEADS * HEAD_DIM
WINDOW = 128
BLK = 128
NUM_BUCKETS = 32
MAX_DISTANCE = 128
RNN_BLOCKS = 8
RNN_BW = 128
D_RNN = RNN_BLOCKS * RNN_BW
CONV_W = 4
CONV_LEFT = 2
LRU_C = 8.0
EPS = 1e-6
NEG_INF = -1e30
LOG2E = math.log2(math.e)

SUBLANES = 8
LANES = 128
ROW_TILE = 512
FFN_TILE = 1024
RNN_SUB = 128
RNN_SUB_OUT = 512
HALO = SUBLANES
FF_CHUNK = 512
VMEM_LIMIT = 58 * 1024 * 1024

assert RNN_BLOCKS == SUBLANES and RNN_BW == LANES and WINDOW == BLK
assert HALO >= max(CONV_LEFT, CONV_W - 1 - CONV_LEFT) and ROW_TILE % RNN_SUB == 0 and ROW_TILE % RNN_SUB_OUT == 0


def _params(n_axes):
    return pltpu.CompilerParams(
        dimension_semantics=("arbitrary",) * n_axes, vmem_limit_bytes=VMEM_LIMIT)


def _resident(shape):
    zeros = (0,) * len(shape)
    return pl.BlockSpec(shape, lambda *_: zeros, pipeline_mode=pl.Buffered(1))


_SMEM = pl.BlockSpec(memory_space=pltpu.SMEM)


def _rms(x, g):
    ms = jnp.mean(x * x, axis=-1, keepdims=True)
    return x * lax.rsqrt(ms + EPS) * g


def _sigmoid(x):
    return 0.5 * jnp.tanh(0.5 * x) + 0.5


def _gelu_tanh(x):
    return 0.5 * x * (1.0 + jnp.tanh(math.sqrt(2.0 / math.pi) * (x + 0.044715 * (x * x * x))))


def _dot(a, b):
    return jnp.dot(a, b, preferred_element_type=F32)


def _block_rows(ref, blk, n):
    return ref.at[pl.ds(blk, n, stride=SUBLANES), :]


def _by_step(v):
    return v.reshape(-1, SUBLANES, LANES)


def _ffn_body(x_ref, g_ref, wg_ref, wu_ref, wd_ref, *rest, final):
    if final:
        fg_ref, o_ref, a_scr = rest
    else:
        o_ref, a_scr = rest
    x = x_ref[...]
    h = _rms(x, g_ref[...]).astype(BF16)
    d_ff = wg_ref.shape[1]
    for s in range(0, d_ff, FF_CHUNK):
        e = min(s + FF_CHUNK, d_ff)
        gate = _dot(h, wg_ref[:, s:e])
        up = _dot(h, wu_ref[:, s:e])
        hg = 0.5 * gate
        a_scr[:, s:e] = ((hg + hg * jnp.tanh(hg)) * up).astype(BF16)
    y = x + 0.5 * _dot(a_scr[...], wd_ref[...])
    if final:
        y = _rms(y, fg_ref[...])
    o_ref[...] = y


def _ffn(x, g, wg, wu, wd, final_g=None):
    n, d = x.shape
    d_ff = wg.shape[1]
    final = final_g is not None
    row = pl.BlockSpec((FFN_TILE, d), lambda i: (i, 0))
    in_specs = [row, _resident((1, d)), _resident((d, d_ff)), _resident((d, d_ff)),
                _resident((d_ff, d))]
    args = [x, g, wg, wu, wd]
    if final:
        in_specs.append(_resident((1, d)))
        args.append(final_g)
    return pl.pallas_call(
        functools.partial(_ffn_body, final=final),
        out_shape=jax.ShapeDtypeStruct((n, d), F32),
        grid=(n // FFN_TILE,),
        in_specs=in_specs,
        out_specs=row,
        scratch_shapes=[pltpu.VMEM((FFN_TILE, d_ff), BF16)],
        compiler_params=_params(1),
        name="ffn_final" if final else "ffn",
    )(*args)


def _rnn_consts(lam_ref, ba_ref, bx_ref):
    neg = -lam_ref[...]
    softplus = jnp.maximum(neg, 0.0) + jnp.log1p(jnp.exp(-jnp.abs(neg)))
    return (-0.5 * LRU_C * LOG2E) * softplus, 0.5 * ba_ref[...], 0.5 * bx_ref[...]


def _conv(window, cw_ref, cb_ref):
    xc = cb_ref[...]
    for tap in range(CONV_W):
        xc = xc + window(tap - CONV_LEFT) * cw_ref[tap]
    return xc


def _rnn_steps(xc_ref, base, n, consts, w_ref, scratch, h, reverse, emit):
    pa_scr, pi_scr = scratch
    k, half_ba, half_bx = consts
    for blk in range(RNN_BLOCKS):
        pre = _dot(_block_rows(xc_ref, base + blk, n)[...].astype(BF16), w_ref[blk])
        _block_rows(pa_scr, blk, n)[...] = pre[:, :RNN_BW]
        _block_rows(pi_scr, blk, n)[...] = pre[:, RNN_BW:]
    yield
    ta = jnp.tanh(_by_step(pa_scr[...]) + half_ba)
    a = jnp.exp2(k * ta + k)
    y = 1.0 - a * a
    gain = y * lax.rsqrt(jnp.maximum(y, 1e-37))
    ti = jnp.tanh(_by_step(pi_scr[...]) + half_bx)
    xc = _by_step(xc_ref[pl.ds(base, n * SUBLANES), :])
    bb = gain * ((ti + 1.0) * (0.5 * xc))
    yield
    for s in (range(n - 1, -1, -1) if reverse else range(n)):
        h = a[s] * h + bb[s]
        emit(s, h)
    yield
    return h


def _interleave(*streams):
    live = list(streams)
    while live:
        for gen in list(live):
            try:
                next(gen)
            except StopIteration:
                live.remove(gen)


def _rnn_scratch(n_sub, steps):
    return [pltpu.VMEM((steps * SUBLANES, LANES), F32)] * (2 * n_sub)


def _t5_bucket_table():
    rel = np.arange(3 * BLK)[:, None] - BLK - np.arange(BLK)[None, :]
    n = NUM_BUCKETS // 2
    max_exact = n // 2
    ret = (rel > 0).astype(np.int32) * n
    na = np.abs(rel)
    large = max_exact + (np.log(np.maximum(na, 1) / max_exact)
                         / math.log(MAX_DISTANCE / max_exact) * (n - max_exact)).astype(np.int32)
    large = np.minimum(large, n - 1)
    bucket = ret + np.where(na < max_exact, na, large)
    return np.where(na <= WINDOW, bucket, -1).astype(np.int32)


def _build_bias(bucket_ref, table_ref, bias_scr):
    bucket = bucket_ref[...]
    row = lax.broadcasted_iota(jnp.int32, bucket.shape, 0)
    for h in range(N_HEADS):
        acc = jnp.full(bucket.shape, NEG_INF, F32)
        for bk in range(NUM_BUCKETS):
            acc = jnp.where(bucket == bk, table_ref[bk, h] * LOG2E, acc)
        g, hh = divmod(h, GROUP)
        c = slice(hh * BLK, (hh + 1) * BLK)
        bias_scr[0, g, :, c] = acc
        bias_scr[1, g, :, c] = jnp.where(row < BLK, NEG_INF, acc)
        bias_scr[2, g, :, c] = jnp.where(row >= 2 * BLK, NEG_INF, acc)


def _attend(q_ref, k_ref, v_ref, j, bias_scr, variant, sink_ref, o_ref):
    k_win = k_ref[j * BLK:(j + 3) * BLK, :]
    vt_win = v_ref[j * BLK:(j + 3) * BLK, :].astype(F32).T.astype(BF16)
    for g in range(N_KV_HEADS):
        heads = range(g * GROUP, (g + 1) * GROUP)
        q = jnp.concatenate(
            [q_ref[j * BLK:(j + 1) * BLK, h * HEAD_DIM:(h + 1) * HEAD_DIM] for h in heads], axis=0)
        k = k_win[:, g * HEAD_DIM:(g + 1) * HEAD_DIM]
        vt = vt_win[g * HEAD_DIM:(g + 1) * HEAD_DIM, :]
        s = lax.dot_general(k, q, (((1,), (1,)), ((), ())), preferred_element_type=F32)
        s = s + bias_scr[variant, g]
        sink = jnp.concatenate(
            [jnp.full((1, BLK), sink_ref[h] * LOG2E, F32) for h in heads], axis=1)
        m = jnp.maximum(jnp.max(s, axis=0, keepdims=True), sink)
        p = jnp.exp2(s - m)
        denom = jnp.sum(p, axis=0, keepdims=True) + jnp.exp2(sink - m)
        o = _dot(vt, p.astype(BF16)) * (1.0 / denom)
        for hh, h in enumerate(heads):
            o_ref[0, j * BLK:(j + 1) * BLK, h * HEAD_DIM:(h + 1) * HEAD_DIM] = (
                o[:, hh * BLK:(hh + 1) * BLK].T.astype(o_ref.dtype))
        yield


PROJ_CHUNK = 256


def _mixer_in_body(bucket_ref, table_ref, sink_ref, x_ref, g_ref, w_ref, cw_ref, cb_ref, lam_ref,
                   wgate_ref, ba_ref, bx_ref,
                   gy_ref, ga_ref, gr_ref, attn_ref, xc_ref, fwd_ref,
                   q_scr, k_scr, v_scr, e_scr, bias_scr, carry_scr, *rnn_scr, n_tiles):
    i = pl.program_id(0)
    tm = x_ref.shape[1]
    d = ga_ref.shape[2]
    halo_rows = HALO * SUBLANES
    main_rows = tm * SUBLANES
    cur_first = i % n_tiles == 0
    lag_first = i % n_tiles == 1
    past_end = cur_first
    new = i % 2
    old = 1 - new
    q_old, k_old, v_old, e_old = q_scr.at[old], k_scr.at[old], v_scr.at[old], e_scr.at[old]
    q_new, k_new, v_new, e_new = q_scr.at[new], k_scr.at[new], v_scr.at[new], e_scr.at[new]

    @pl.when(i == 0)
    def _():
        _build_bias(bucket_ref, table_ref, bias_scr)
        q_scr[...] = jnp.zeros_like(q_scr)
        k_scr[...] = jnp.zeros_like(k_scr)
        v_scr[...] = jnp.zeros_like(v_scr)
        e_scr[...] = jnp.zeros_like(e_scr)
        carry_scr[...] = jnp.zeros_like(carry_scr)

    h = _rms(x_ref[0], g_ref[...]).astype(BF16)

    def projection():
        col = {"q": 0, "k": Q_W, "v": Q_W + KV_W, "xr": Q_W + 2 * KV_W,
               "gy": Q_W + 2 * KV_W + D_RNN, "ga": Q_W + 2 * KV_W + 2 * D_RNN,
               "gr": Q_W + 2 * KV_W + 2 * D_RNN + d}
        def finish(name, c, chunk, y):
            out = slice(c, c + chunk)
            if name == "k":
                k_new[BLK:BLK + tm, out] = y.astype(BF16)
                k_old[BLK + tm:, out] = y[:BLK].astype(BF16)
            elif name == "v":
                v_new[BLK:BLK + tm, out] = y.astype(BF16)
                v_old[BLK + tm:, out] = y[:BLK].astype(BF16)
            elif name == "xr":
                for blk in range(c // LANES, (c + chunk) // LANES):
                    part = y[:, blk * LANES - c:(blk + 1) * LANES - c]
                    e_new[pl.ds(halo_rows + blk, tm, stride=SUBLANES), :] = part
                    e_old[pl.ds(halo_rows + main_rows + blk, HALO, stride=SUBLANES), :] = (
                        jnp.where(past_end, 0.0, part[:HALO]))
            elif name == "q":
                q_new[:, out] = (y * (HEAD_DIM ** -0.5 * LOG2E)).astype(BF16)
            elif name == "gy":
                gy_ref[0, :, out] = _gelu_tanh(y).astype(gy_ref.dtype)
            elif name == "ga":
                ga_ref[0, :, out] = y.astype(ga_ref.dtype)
            else:
                gr_ref[0, :, out] = y.astype(gr_ref.dtype)

        pending = None
        for name, width in (("k", KV_W), ("v", KV_W), ("xr", D_RNN), ("q", Q_W),
                            ("gy", D_RNN), ("ga", d), ("gr", d)):
            chunk = min(PROJ_CHUNK, width)
            for c in range(0, width, chunk):
                y = _dot(h, w_ref[:, col[name] + c:col[name] + c + chunk])
                if pending is not None:
                    finish(*pending)
                    yield
                pending = (name, c, chunk, y)
        finish(*pending)
        yield

    def lagged():
        consts = _rnn_consts(lam_ref, ba_ref, bx_ref)
        hstate = jnp.where(lag_first, 0.0, carry_scr[...])
        n_blk = tm // BLK
        for sub in range(tm // RNN_SUB):
            lo = HALO + sub * RNN_SUB

            def window(off, lo=lo):
                return _by_step(e_old[pl.ds((lo + off) * SUBLANES, RNN_SUB * SUBLANES), :])

            def emit(st, hv, sub=sub):
                fwd_ref[0, pl.ds((sub * RNN_SUB + st) * SUBLANES, SUBLANES), :] = hv

            base = sub * RNN_SUB * SUBLANES
            xc_ref[0, pl.ds(base, RNN_SUB * SUBLANES), :] = (
                _conv(window, cw_ref, cb_ref).reshape(RNN_SUB * SUBLANES, LANES))
            yield
            hstate = yield from _rnn_steps(xc_ref.at[0], base, RNN_SUB, consts, wgate_ref,
                                           rnn_scr[2 * sub:2 * sub + 2], hstate, False, emit)
            for j in range(sub * RNN_SUB // BLK, (sub + 1) * RNN_SUB // BLK):
                if j == 0:
                    variant = jnp.where(lag_first, 1, 0)
                elif j == n_blk - 1:
                    variant = jnp.where(past_end, 2, 0)
                else:
                    variant = 0
                yield from _attend(q_old, k_old, v_old, j, bias_scr, variant, sink_ref, attn_ref)
        carry_scr[...] = hstate

    k_new[0:BLK, :] = k_old[tm:tm + BLK, :]
    v_new[0:BLK, :] = v_old[tm:tm + BLK, :]
    e_new[0:halo_rows, :] = jnp.where(cur_first, 0.0, e_old[main_rows:main_rows + halo_rows, :])

    _interleave(projection(), lagged())


def _mixer_in(x, p, l):
    bsz, seq, d = x.shape
    tm = ROW_TILE
    assert seq % tm == 0 and seq // tm >= 2
    n_tiles = seq // tm
    bucket = jnp.asarray(_t5_bucket_table())

    total = bsz * n_tiles

    def cur(i):
        flat = jnp.minimum(i, total - 1)
        return (flat // n_tiles, flat % n_tiles, 0)

    def lag(i):
        flat = jnp.maximum(i - 1, 0)
        return (flat // n_tiles, flat % n_tiles, 0)

    step_major = jax.ShapeDtypeStruct((bsz, seq * SUBLANES, LANES), F32)
    outs = [
        (jax.ShapeDtypeStruct((bsz, seq, D_RNN), BF16), pl.BlockSpec((1, tm, D_RNN), cur)),
        (jax.ShapeDtypeStruct((bsz, seq, d), BF16), pl.BlockSpec((1, tm, d), cur)),
        (jax.ShapeDtypeStruct((bsz, seq, d), BF16), pl.BlockSpec((1, tm, d), cur)),
        (jax.ShapeDtypeStruct((bsz, seq, Q_W), BF16), pl.BlockSpec((1, tm, Q_W), lag)),
        (step_major, pl.BlockSpec((1, tm * SUBLANES, LANES), lag)),
        (step_major, pl.BlockSpec((1, tm * SUBLANES, LANES), lag)),
    ]
    weights = [p["mix_norm"][l], p["w_in"][l], p["conv_w"][l], p["conv_b"][l], p["lam"][l, 0],
               p["w_gate"][l, 0], p["b_a"][l, 0], p["b_x"][l, 0]]
    return pl.pallas_call(
        functools.partial(_mixer_in_body, n_tiles=n_tiles),
        out_shape=[o[0] for o in outs],
        grid=(total + 1,),
        in_specs=[_resident(bucket.shape), _SMEM, _SMEM, pl.BlockSpec((1, tm, d), cur)]
        + [_resident(w.shape) for w in weights],
        out_specs=[o[1] for o in outs],
        scratch_shapes=[pltpu.VMEM((2, tm, Q_W), BF16),
                        pltpu.VMEM((2, tm + 2 * BLK, KV_W), BF16),
                        pltpu.VMEM((2, tm + 2 * BLK, KV_W), BF16),
                        pltpu.VMEM((2, (tm + 2 * HALO) * SUBLANES, LANES), F32),
                        pltpu.VMEM((3, N_KV_HEADS, 3 * BLK, GROUP * BLK), F32),
                        pltpu.VMEM((SUBLANES, LANES), F32)]
        + _rnn_scratch(tm // RNN_SUB, RNN_SUB),
        compiler_params=_params(1),
        name="mixer_in",
    )(bucket, p["rel_bias_table"], p["attn_sink"][l], x, *weights)


def _mixer_out_body(x_ref, attn_ref, xc_ref, fwd_ref, gy_ref, ga_ref, gr_ref,
                    lam_ref, wgate_ref, ba_ref, bx_ref, wa_ref, wr_ref, wo_ref,
                    o_ref, rec_scr, ya_scr, carry_scr, *rnn_scr):
    t = pl.program_id(1)
    tm = x_ref.shape[1]

    @pl.when(t == 0)
    def _():
        carry_scr[...] = jnp.zeros_like(carry_scr)

    def attn_branch():
        for c in range(0, wa_ref.shape[1], PROJ_CHUNK):
            ya_scr[:, c:c + PROJ_CHUNK] = _dot(attn_ref[0], wa_ref[:, c:c + PROJ_CHUNK])
            yield

    def scan():
        consts = _rnn_consts(lam_ref, ba_ref, bx_ref)
        hstate = carry_scr[...]
        for sub in range(tm // RNN_SUB_OUT - 1, -1, -1):
            base = sub * RNN_SUB_OUT * SUBLANES

            def emit(st, hv, base=base):
                r = pl.ds(base + st * SUBLANES, SUBLANES)
                rec_scr[r, :] = fwd_ref[0, r, :] + hv

            hstate = yield from _rnn_steps(xc_ref.at[0], base, RNN_SUB_OUT, consts, wgate_ref,
                                           rnn_scr[2 * sub:2 * sub + 2], hstate, True, emit)
        carry_scr[...] = hstate

    _interleave(scan(), attn_branch())

    rnn = jnp.concatenate(
        [(_block_rows(rec_scr, blk, tm)[...]
          * gy_ref[0, :, blk * LANES:(blk + 1) * LANES].astype(F32)).astype(BF16)
         for blk in range(RNN_BLOCKS)], axis=1)
    yr = _dot(rnn, wr_ref[...])
    merged = (_sigmoid(ga_ref[0].astype(F32)) * ya_scr[...]
              + _sigmoid(gr_ref[0].astype(F32)) * yr)
    o_ref[0] = x_ref[0] + _dot(merged.astype(BF16), wo_ref[...])


def _mixer_out(x, attn, xc, fwd, gy, ga, gr, p, l):
    bsz, seq, d = x.shape
    tm = ROW_TILE
    n_tiles = seq // tm

    def tile(b, t):
        return (b, n_tiles - 1 - t, 0)

    def row(w):
        return pl.BlockSpec((1, tm, w), tile)

    main = pl.BlockSpec((1, tm * SUBLANES, LANES), tile)
    weights = [p["lam"][l, 1], p["w_gate"][l, 1], p["b_a"][l, 1], p["b_x"][l, 1],
               p["w_br_attn"][l], p["w_br_rnn"][l], p["w_out"][l]]
    return pl.pallas_call(
        _mixer_out_body,
        out_shape=jax.ShapeDtypeStruct(x.shape, F32),
        grid=(bsz, n_tiles),
        in_specs=[row(d), row(Q_W), main, main, row(D_RNN), row(d), row(d)]
        + [_resident(w.shape) for w in weights],
        out_specs=row(d),
        scratch_shapes=[pltpu.VMEM((tm * SUBLANES, LANES), F32), pltpu.VMEM((tm, d), F32),
                        pltpu.VMEM((SUBLANES, LANES), F32)]
        + _rnn_scratch(tm // RNN_SUB_OUT, RNN_SUB_OUT),
        compiler_params=_params(2),
        name="mixer_out",
    )(x, attn, xc, fwd, gy, ga, gr, *weights)


def _trunk(x, p):
    bsz, seq, d = x.shape
    depth = p["w_in"].shape[0]

    def ffn(x, which, l, final_g=None):
        y = _ffn(x.reshape(bsz * seq, d), p[which + "_norm"][l], p[which + "_wg"][l],
                 p[which + "_wu"][l], p[which + "_wd"][l], final_g=final_g)
        return y.reshape(bsz, seq, d)

    for l in range(depth):
        x = ffn(x, "ffn1", l)
        gy, ga, gr, attn, xc, fwd = _mixer_in(x, p, l)
        x = _mixer_out(x, attn, xc, fwd, gy, ga, gr, p, l)
        x = ffn(x, "ffn2", l, final_g=p["final_norm"] if l == depth - 1 else None)
    return x


def kernel(x_prompt, x_sample, ffn1_norm, ffn1_w_up, ffn1_w_down, mix_norm, w_in, conv_w, conv_b,
           rg_lambda, rg_w_a, rg_b_a, rg_w_x, rg_b_x, attn_sink, rel_bias_table, w_br_attn,
           w_br_rnn, w_out, ffn2_norm, ffn2_w_up, ffn2_w_down, final_norm):
    depth, d = ffn1_norm.shape
    d_ff = ffn1_w_down.shape[1]
    assert conv_w.shape[2] == D_RNN and w_in.shape[2] == Q_W + 2 * KV_W + 2 * D_RNN + 2 * d
    blocked = (RNN_BLOCKS, RNN_BW)
    p = {
        "ffn1_norm": ffn1_norm.reshape(depth, 1, d),
        "ffn1_wg": ffn1_w_up[:, :, :d_ff].astype(BF16),
        "ffn1_wu": ffn1_w_up[:, :, d_ff:].astype(BF16),
        "ffn1_wd": ffn1_w_down.astype(BF16),
        "mix_norm": mix_norm.reshape(depth, 1, d),
        "w_in": w_in.astype(BF16),
        "conv_w": conv_w.reshape(depth, CONV_W, *blocked),
        "conv_b": conv_b.reshape(depth, *blocked),
        "lam": rg_lambda.reshape(depth, 2, *blocked),
        "w_gate": (0.5 * jnp.concatenate([rg_w_a, rg_w_x], axis=-1)).astype(BF16),
        "b_a": rg_b_a.reshape(depth, 2, *blocked),
        "b_x": rg_b_x.reshape(depth, 2, *blocked),
        "attn_sink": attn_sink,
        "rel_bias_table": rel_bias_table,
        "w_br_attn": w_br_attn.astype(BF16),
        "w_br_rnn": w_br_rnn.astype(BF16),
        "w_out": w_out.astype(BF16),
        "ffn2_norm": ffn2_norm.reshape(depth, 1, d),
        "ffn2_wg": ffn2_w_up[:, :, :d_ff].astype(BF16),
        "ffn2_wu": ffn2_w_up[:, :, d_ff:].astype(BF16),
        "ffn2_wd": ffn2_w_down.astype(BF16),
        "final_norm": final_norm.reshape(1, d),
    }
    return (_trunk(x_prompt, p), _trunk(x_sample, p))
```

```python
import functools
import math

import numpy as np
import jax
import jax.numpy as jnp
from jax import lax
from jax.experimental import pallas as pl
from jax.experimental.pallas import tpu as pltpu

F32 = jnp.float32
BF16 = jnp.bfloat16

N_HEADS = 8
HEAD_DIM = 128
N_KV_HEADS = 2
GROUP = N_HEADS // N_KV_HEADS
Q_W = N_HEADS * HEAD_DIM
KV_W = N_KV_HEADS * HEAD_DIM
WINDOW = 128
BLK = 128
NUM_BUCKETS = 32
MAX_DISTANCE = 128
RNN_BLOCKS = 8
RNN_BW = 128
D_RNN = RNN_BLOCKS * RNN_BW
CONV_W = 4
CONV_LEFT = 2
LRU_C = 8.0
EPS = 1e-6
NEG_INF = -1e30
LOG2E = math.log2(math.e)

SUBLANES = 8
LANES = 128
ROW_TILE = 512
FFN_TILE = 1024
RNN_SUB = 128
RNN_SUB_OUT = 512
HALO = SUBLANES
FF_CHUNK = 512
VMEM_LIMIT = 58 * 1024 * 1024

assert RNN_BLOCKS == SUBLANES and RNN_BW == LANES and WINDOW == BLK
assert HALO >= max(CONV_LEFT, CONV_W - 1 - CONV_LEFT) and ROW_TILE % RNN_SUB == 0 and ROW_TILE % RNN_SUB_OUT == 0


def _params(n_axes):
    return pltpu.CompilerParams(
        dimension_semantics=("arbitrary",) * n_axes, vmem_limit_bytes=VMEM_LIMIT)


def _resident(shape):
    zeros = (0,) * len(shape)
    return pl.BlockSpec(shape, lambda *_: zeros, pipeline_mode=pl.Buffered(1))


_SMEM = pl.BlockSpec(memory_space=pltpu.SMEM)


def _rms(x, g):
    ms = jnp.mean(x * x, axis=-1, keepdims=True)
    return x * lax.rsqrt(ms + EPS) * g


def _sigmoid(x):
    return 0.5 * jnp.tanh(0.5 * x) + 0.5


def _gelu_tanh(x):
    return 0.5 * x * (1.0 + jnp.tanh(math.sqrt(2.0 / math.pi) * (x + 0.044715 * (x * x * x))))


def _dot(a, b):
    return jnp.dot(a, b, preferred_element_type=F32)


def _block_rows(ref, blk, n):
    return ref.at[pl.ds(blk, n, stride=SUBLANES), :]


def _by_step(v):
    return v.reshape(-1, SUBLANES, LANES)


def _ffn_body(x_ref, g_ref, wg_ref, wu_ref, wd_ref, *rest, final):
    if final:
        fg_ref, o_ref, a_scr = rest
    else:
        o_ref, a_scr = rest
    x = x_ref[...]
    h = _rms(x, g_ref[...]).astype(BF16)
    d_ff = wg_ref.shape[1]
    for s in range(0, d_ff, FF_CHUNK):
        e = min(s + FF_CHUNK, d_ff)
        gate = _dot(h, wg_ref[:, s:e])
        up = _dot(h, wu_ref[:, s:e])
        hg = 0.5 * gate
        a_scr[:, s:e] = ((hg + hg * jnp.tanh(hg)) * up).astype(BF16)
    y = x + 0.5 * _dot(a_scr[...], wd_ref[...])
    if final:
        y = _rms(y, fg_ref[...])
    o_ref[...] = y


def _ffn(x, g, wg, wu, wd, final_g=None):
    n, d = x.shape
    d_ff = wg.shape[1]
    final = final_g is not None
    row = pl.BlockSpec((FFN_TILE, d), lambda i: (i, 0))
    in_specs = [row, _resident((1, d)), _resident((d, d_ff)), _resident((d, d_ff)),
                _resident((d_ff, d))]
    args = [x, g, wg, wu, wd]
    if final:
        in_specs.append(_resident((1, d)))
        args.append(final_g)
    return pl.pallas_call(
        functools.partial(_ffn_body, final=final),
        out_shape=jax.ShapeDtypeStruct((n, d), F32),
        grid=(n // FFN_TILE,),
        in_specs=in_specs,
        out_specs=row,
        scratch_shapes=[pltpu.VMEM((FFN_TILE, d_ff), BF16)],
        compiler_params=_params(1),
        name="ffn_final" if final else "ffn",
    )(*args)


def _rnn_consts(lam_ref):
    neg = -lam_ref[...]
    softplus = jnp.maximum(neg, 0.0) + jnp.log1p(jnp.exp(-jnp.abs(neg)))
    return (-0.5 * LRU_C * LOG2E) * softplus


def _conv(window, cw_ref, cb_ref):
    xc = cb_ref[...]
    for tap in range(CONV_W):
        xc = xc + window(tap - CONV_LEFT) * cw_ref[tap]
    return xc


def _rnn_steps(xc_ref, base, n, consts, w_ref, scratch, h, reverse, emit):
    pa_scr, pi_scr = scratch
    k = consts
    ones = jnp.where(lax.broadcasted_iota(jnp.int32, (n, RNN_BW), 1) == 0, 1.0, 0.0).astype(BF16)
    for blk in range(RNN_BLOCKS):
        lhs = jnp.concatenate([_block_rows(xc_ref, base + blk, n)[...].astype(BF16), ones], axis=1)
        pre = _dot(lhs, w_ref[blk])
        _block_rows(pa_scr, blk, n)[...] = pre[:, :RNN_BW]
        _block_rows(pi_scr, blk, n)[...] = pre[:, RNN_BW:]
    yield
    ta = jnp.tanh(_by_step(pa_scr[...]))
    a = jnp.exp2(k * ta + k)
    y = 1.0 - a * a
    gain = y * lax.rsqrt(jnp.maximum(y, 1e-37))
    ti = jnp.tanh(_by_step(pi_scr[...]))
    xc = _by_step(xc_ref[pl.ds(base, n * SUBLANES), :])
    bb = gain * ((ti + 1.0) * (0.5 * xc))
    yield
    for s in (range(n - 1, -1, -1) if reverse else range(n)):
        h = a[s] * h + bb[s]
        emit(s, h)
    yield
    return h


def _interleave(*streams):
    live = list(streams)
    while live:
        for gen in list(live):
            try:
                next(gen)
            except StopIteration:
                live.remove(gen)


def _rnn_scratch(n_sub, steps):
    return [pltpu.VMEM((steps * SUBLANES, LANES), F32)] * (2 * n_sub)


def _t5_bucket_table():
    rel = np.arange(3 * BLK)[:, None] - BLK - np.arange(BLK)[None, :]
    n = NUM_BUCKETS // 2
    max_exact = n // 2
    ret = (rel > 0).astype(np.int32) * n
    na = np.abs(rel)
    large = max_exact + (np.log(np.maximum(na, 1) / max_exact)
                         / math.log(MAX_DISTANCE / max_exact) * (n - max_exact)).astype(np.int32)
    large = np.minimum(large, n - 1)
    bucket = ret + np.where(na < max_exact, na, large)
    return np.where(na <= WINDOW, bucket, -1).astype(np.int32)


def _build_bias(bucket_ref, table_ref, bias_scr):
    bucket = bucket_ref[...]
    row = lax.broadcasted_iota(jnp.int32, bucket.shape, 0)
    for h in range(N_HEADS):
        acc = jnp.full(bucket.shape, NEG_INF, F32)
        for bk in range(NUM_BUCKETS):
            acc = jnp.where(bucket == bk, table_ref[bk, h] * LOG2E, acc)
        g, hh = divmod(h, GROUP)
        c = slice(hh * BLK, (hh + 1) * BLK)
        bias_scr[0, g, :, c] = acc
        bias_scr[1, g, :, c] = jnp.where(row < BLK, NEG_INF, acc)
        bias_scr[2, g, :, c] = jnp.where(row >= 2 * BLK, NEG_INF, acc)


def _attend(q_ref, k_ref, v_ref, j, bias_scr, variant, sink_ref, o_ref):
    k_win = k_ref[j * BLK:(j + 3) * BLK, :]
    vt_win = v_ref[j * BLK:(j + 3) * BLK, :].astype(F32).T.astype(BF16)
    for g in range(N_KV_HEADS):
        heads = range(g * GROUP, (g + 1) * GROUP)
        q = jnp.concatenate(
            [q_ref[j * BLK:(j + 1) * BLK, h * HEAD_DIM:(h + 1) * HEAD_DIM] for h in heads], axis=0)
        k = k_win[:, g * HEAD_DIM:(g + 1) * HEAD_DIM]
        vt = vt_win[g * HEAD_DIM:(g + 1) * HEAD_DIM, :]
        s = lax.dot_general(k, q, (((1,), (1,)), ((), ())), preferred_element_type=F32)
        s = s + bias_scr[variant, g]
        sink = jnp.concatenate(
            [jnp.full((1, BLK), sink_ref[h] * LOG2E, F32) for h in heads], axis=1)
        m = jnp.maximum(jnp.max(s, axis=0, keepdims=True), sink)
        p = jnp.exp2(s - m)
        denom = jnp.sum(p, axis=0, keepdims=True) + jnp.exp2(sink - m)
        o = _dot(vt, p.astype(BF16)) * (1.0 / denom)
        for hh, h in enumerate(heads):
            o_ref[0, j * BLK:(j + 1) * BLK, h * HEAD_DIM:(h + 1) * HEAD_DIM] = (
                o[:, hh * BLK:(hh + 1) * BLK].T.astype(o_ref.dtype))
        yield


PROJ_CHUNK = 256


def _mixer_in_body(bucket_ref, table_ref, sink_ref, x_ref, g_ref, w_ref, cw_ref, cb_ref, lam_ref,
                   wgate_ref,
                   gy_ref, ga_ref, gr_ref, attn_ref, xc_ref, fwd_ref,
                   q_scr, k_scr, v_scr, e_scr, bias_scr, carry_scr, *rnn_scr, n_tiles):
    i = pl.program_id(0)
    tm = x_ref.shape[1]
    d = ga_ref.shape[2]
    halo_rows = HALO * SUBLANES
    main_rows = tm * SUBLANES
    cur_first = i % n_tiles == 0
    lag_first = i % n_tiles == 1
    past_end = cur_first
    new = i % 2
    old = 1 - new
    q_old, k_old, v_old, e_old = q_scr.at[old], k_scr.at[old], v_scr.at[old], e_scr.at[old]
    q_new, k_new, v_new, e_new = q_scr.at[new], k_scr.at[new], v_scr.at[new], e_scr.at[new]

    @pl.when(i == 0)
    def _():
        _build_bias(bucket_ref, table_ref, bias_scr)
        q_scr[...] = jnp.zeros_like(q_scr)
        k_scr[...] = jnp.zeros_like(k_scr)
        v_scr[...] = jnp.zeros_like(v_scr)
        e_scr[...] = jnp.zeros_like(e_scr)
        carry_scr[...] = jnp.zeros_like(carry_scr)

    h = _rms(x_ref[0], g_ref[...]).astype(BF16)

    def projection():
        col = {"q": 0, "k": Q_W, "v": Q_W + KV_W, "xr": Q_W + 2 * KV_W,
               "gy": Q_W + 2 * KV_W + D_RNN, "ga": Q_W + 2 * KV_W + 2 * D_RNN,
               "gr": Q_W + 2 * KV_W + 2 * D_RNN + d}
        def finish(name, c, chunk, y):
            out = slice(c, c + chunk)
            if name == "k":
                k_new[BLK:BLK + tm, out] = y.astype(BF16)
                k_old[BLK + tm:, out] = y[:BLK].astype(BF16)
            elif name == "v":
                v_new[BLK:BLK + tm, out] = y.astype(BF16)
                v_old[BLK + tm:, out] = y[:BLK].astype(BF16)
            elif name == "xr":
                for blk in range(c // LANES, (c + chunk) // LANES):
                    part = y[:, blk * LANES - c:(blk + 1) * LANES - c]
                    e_new[pl.ds(halo_rows + blk, tm, stride=SUBLANES), :] = part
                    e_old[pl.ds(halo_rows + main_rows + blk, HALO, stride=SUBLANES), :] = (
                        jnp.where(past_end, 0.0, part[:HALO]))
            elif name == "q":
                q_new[:, out] = (y * (HEAD_DIM ** -0.5 * LOG2E)).astype(BF16)
            elif name == "gy":
                gy_ref[0, :, out] = _gelu_tanh(y).astype(gy_ref.dtype)
            elif name == "ga":
                ga_ref[0, :, out] = y.astype(ga_ref.dtype)
            else:
                gr_ref[0, :, out] = y.astype(gr_ref.dtype)

        pending = None
        for name, width in (("k", KV_W), ("v", KV_W), ("xr", D_RNN), ("q", Q_W),
                            ("gy", D_RNN), ("ga", d), ("gr", d)):
            chunk = min(PROJ_CHUNK, width)
            for c in range(0, width, chunk):
                y = _dot(h, w_ref[:, col[name] + c:col[name] + c + chunk])
                if pending is not None:
                    finish(*pending)
                    yield
                pending = (name, c, chunk, y)
        finish(*pending)
        yield

    def lagged():
        consts = _rnn_consts(lam_ref)
        hstate = jnp.where(lag_first, 0.0, carry_scr[...])
        n_blk = tm // BLK
        for sub in range(tm // RNN_SUB):
            lo = HALO + sub * RNN_SUB

            def window(off, lo=lo):
                return _by_step(e_old[pl.ds((lo + off) * SUBLANES, RNN_SUB * SUBLANES), :])

            def emit(st, hv, sub=sub):
                fwd_ref[0, pl.ds((sub * RNN_SUB + st) * SUBLANES, SUBLANES), :] = hv

            base = sub * RNN_SUB * SUBLANES
            xc_ref[0, pl.ds(base, RNN_SUB * SUBLANES), :] = (
                _conv(window, cw_ref, cb_ref).reshape(RNN_SUB * SUBLANES, LANES))
            yield
            hstate = yield from _rnn_steps(xc_ref.at[0], base, RNN_SUB, consts, wgate_ref,
                                           rnn_scr[2 * sub:2 * sub + 2], hstate, False, emit)
            for j in range(sub * RNN_SUB // BLK, (sub + 1) * RNN_SUB // BLK):
                if j == 0:
                    variant = jnp.where(lag_first, 1, 0)
                elif j == n_blk - 1:
                    variant = jnp.where(past_end, 2, 0)
                else:
                    variant = 0
                yield from _attend(q_old, k_old, v_old, j, bias_scr, variant, sink_ref, attn_ref)
        carry_scr[...] = hstate

    k_new[0:BLK, :] = k_old[tm:tm + BLK, :]
    v_new[0:BLK, :] = v_old[tm:tm + BLK, :]
    e_new[0:halo_rows, :] = jnp.where(cur_first, 0.0, e_old[main_rows:main_rows + halo_rows, :])

    _interleave(projection(), lagged())


def _mixer_in(x, p, l):
    bsz, seq, d = x.shape
    tm = ROW_TILE
    assert seq % tm == 0 and seq // tm >= 2
    n_tiles = seq // tm
    bucket = jnp.asarray(_t5_bucket_table())

    total = bsz * n_tiles

    def cur(i):
        flat = jnp.minimum(i, total - 1)
        return (flat // n_tiles, flat % n_tiles, 0)

    def lag(i):
        flat = jnp.maximum(i - 1, 0)
        return (flat // n_tiles, flat % n_tiles, 0)

    step_major = jax.ShapeDtypeStruct((bsz, seq * SUBLANES, LANES), F32)
    outs = [
        (jax.ShapeDtypeStruct((bsz, seq, D_RNN), BF16), pl.BlockSpec((1, tm, D_RNN), cur)),
        (jax.ShapeDtypeStruct((bsz, seq, d), BF16), pl.BlockSpec((1, tm, d), cur)),
        (jax.ShapeDtypeStruct((bsz, seq, d), BF16), pl.BlockSpec((1, tm, d), cur)),
        (jax.ShapeDtypeStruct((bsz, seq, Q_W), BF16), pl.BlockSpec((1, tm, Q_W), lag)),
        (step_major, pl.BlockSpec((1, tm * SUBLANES, LANES), lag)),
        (step_major, pl.BlockSpec((1, tm * SUBLANES, LANES), lag)),
    ]
    weights = [p["mix_norm"][l], p["w_in"][l], p["conv_w"][l], p["conv_b"][l], p["lam"][l, 0],
               p["w_gate"][l, 0]]
    return pl.pallas_call(
        functools.partial(_mixer_in_body, n_tiles=n_tiles),
        out_shape=[o[0] for o in outs],
        grid=(total + 1,),
        in_specs=[_resident(bucket.shape), _SMEM, _SMEM, pl.BlockSpec((1, tm, d), cur)]
        + [_resident(w.shape) for w in weights],
        out_specs=[o[1] for o in outs],
        scratch_shapes=[pltpu.VMEM((2, tm, Q_W), BF16),
                        pltpu.VMEM((2, tm + 2 * BLK, KV_W), BF16),
                        pltpu.VMEM((2, tm + 2 * BLK, KV_W), BF16),
                        pltpu.VMEM((2, (tm + 2 * HALO) * SUBLANES, LANES), F32),
                        pltpu.VMEM((3, N_KV_HEADS, 3 * BLK, GROUP * BLK), F32),
                        pltpu.VMEM((SUBLANES, LANES), F32)]
        + _rnn_scratch(tm // RNN_SUB, RNN_SUB),
        compiler_params=_params(1),
        name="mixer_in",
    )(bucket, p["rel_bias_table"], p["attn_sink"][l], x, *weights)


def _mixer_out_body(x_ref, attn_ref, xc_ref, fwd_ref, gy_ref, ga_ref, gr_ref,
                    lam_ref, wgate_ref, wa_ref, wr_ref, wo_ref,
                    o_ref, rec_scr, ya_scr, carry_scr, *rnn_scr):
    t = pl.program_id(1)
    tm = x_ref.shape[1]

    @pl.when(t == 0)
    def _():
        carry_scr[...] = jnp.zeros_like(carry_scr)

    def attn_branch():
        for c in range(0, wa_ref.shape[1], PROJ_CHUNK):
            ya_scr[:, c:c + PROJ_CHUNK] = _dot(attn_ref[0], wa_ref[:, c:c + PROJ_CHUNK])
            yield

    def scan():
        consts = _rnn_consts(lam_ref)
        hstate = carry_scr[...]
        for sub in range(tm // RNN_SUB_OUT - 1, -1, -1):
            base = sub * RNN_SUB_OUT * SUBLANES

            def emit(st, hv, base=base):
                r = pl.ds(base + st * SUBLANES, SUBLANES)
                rec_scr[r, :] = fwd_ref[0, r, :] + hv

            hstate = yield from _rnn_steps(xc_ref.at[0], base, RNN_SUB_OUT, consts, wgate_ref,
                                           rnn_scr[2 * sub:2 * sub + 2], hstate, True, emit)
        carry_scr[...] = hstate

    _interleave(scan(), attn_branch())

    rnn = jnp.concatenate(
        [(_block_rows(rec_scr, blk, tm)[...]
          * gy_ref[0, :, blk * LANES:(blk + 1) * LANES].astype(F32)).astype(BF16)
         for blk in range(RNN_BLOCKS)], axis=1)
    yr = _dot(rnn, wr_ref[...])
    merged = (_sigmoid(ga_ref[0].astype(F32)) * ya_scr[...]
              + _sigmoid(gr_ref[0].astype(F32)) * yr)
    o_ref[0] = x_ref[0] + _dot(merged.astype(BF16), wo_ref[...])


def _mixer_out(x, attn, xc, fwd, gy, ga, gr, p, l):
    bsz, seq, d = x.shape
    tm = ROW_TILE
    n_tiles = seq // tm

    def tile(b, t):
        return (b, n_tiles - 1 - t, 0)

    def row(w):
        return pl.BlockSpec((1, tm, w), tile)

    main = pl.BlockSpec((1, tm * SUBLANES, LANES), tile)
    weights = [p["lam"][l, 1], p["w_gate"][l, 1],
               p["w_br_attn"][l], p["w_br_rnn"][l], p["w_out"][l]]
    return pl.pallas_call(
        _mixer_out_body,
        out_shape=jax.ShapeDtypeStruct(x.shape, F32),
        grid=(bsz, n_tiles),
        in_specs=[row(d), row(Q_W), main, main, row(D_RNN), row(d), row(d)]
        + [_resident(w.shape) for w in weights],
        out_specs=row(d),
        scratch_shapes=[pltpu.VMEM((tm * SUBLANES, LANES), F32), pltpu.VMEM((tm, d), F32),
                        pltpu.VMEM((SUBLANES, LANES), F32)]
        + _rnn_scratch(tm // RNN_SUB_OUT, RNN_SUB_OUT),
        compiler_params=_params(2),
        name="mixer_out",
    )(x, attn, xc, fwd, gy, ga, gr, *weights)


def _trunk(x, p):
    bsz, seq, d = x.shape
    depth = p["w_in"].shape[0]

    def ffn(x, which, l, final_g=None):
        y = _ffn(x.reshape(bsz * seq, d), p[which + "_norm"][l], p[which + "_wg"][l],
                 p[which + "_wu"][l], p[which + "_wd"][l], final_g=final_g)
        return y.reshape(bsz, seq, d)

    for l in range(depth):
        x = ffn(x, "ffn1", l)
        gy, ga, gr, attn, xc, fwd = _mixer_in(x, p, l)
        x = _mixer_out(x, attn, xc, fwd, gy, ga, gr, p, l)
        x = ffn(x, "ffn2", l, final_g=p["final_norm"] if l == depth - 1 else None)
    return x


def kernel(x_prompt, x_sample, ffn1_norm, ffn1_w_up, ffn1_w_down, mix_norm, w_in, conv_w, conv_b,
           rg_lambda, rg_w_a, rg_b_a, rg_w_x, rg_b_x, attn_sink, rel_bias_table, w_br_attn,
           w_br_rnn, w_out, ffn2_norm, ffn2_w_up, ffn2_w_down, final_norm):
    depth, d = ffn1_norm.shape
    d_ff = ffn1_w_down.shape[1]
    assert conv_w.shape[2] == D_RNN and w_in.shape[2] == Q_W + 2 * KV_W + 2 * D_RNN + 2 * d
    blocked = (RNN_BLOCKS, RNN_BW)
    p = {
        "ffn1_norm": ffn1_norm.reshape(depth, 1, d),
        "ffn1_wg": ffn1_w_up[:, :, :d_ff].astype(BF16),
        "ffn1_wu": ffn1_w_up[:, :, d_ff:].astype(BF16),
        "ffn1_wd": ffn1_w_down.astype(BF16),
        "mix_norm": mix_norm.reshape(depth, 1, d),
        "w_in": w_in.astype(BF16),
        "conv_w": conv_w.reshape(depth, CONV_W, *blocked),
        "conv_b": conv_b.reshape(depth, *blocked),
        "lam": rg_lambda.reshape(depth, 2, *blocked),
        "w_gate": jnp.concatenate(
            [0.5 * jnp.concatenate([rg_w_a, rg_w_x], axis=-1),
             0.5 * jnp.concatenate([rg_b_a.reshape(depth, 2, RNN_BLOCKS, 1, RNN_BW),
                                    rg_b_x.reshape(depth, 2, RNN_BLOCKS, 1, RNN_BW)], axis=-1),
             jnp.zeros((depth, 2, RNN_BLOCKS, RNN_BW - 1, 2 * RNN_BW), F32)],
            axis=-2).astype(BF16),
        "attn_sink": attn_sink,
        "rel_bias_table": rel_bias_table,
        "w_br_attn": w_br_attn.astype(BF16),
        "w_br_rnn": w_br_rnn.astype(BF16),
        "w_out": w_out.astype(BF16),
        "ffn2_norm": ffn2_norm.reshape(depth, 1, d),
        "ffn2_wg": ffn2_w_up[:, :, :d_ff].astype(BF16),
        "ffn2_wu": ffn2_w_up[:, :, d_ff:].astype(BF16),
        "ffn2_wd": ffn2_w_down.astype(BF16),
        "final_norm": final_norm.reshape(1, d),
    }
    return (_trunk(x_prompt, p), _trunk(x_sample, p))
```
